```python
import jax, jax.numpy as jnp
from jax import lax
import numpy as np

D_MODEL = 1024
BATCH = 1
SEQ = 16384
DEPTH = 4

N_EVEN = (DEPTH + 1) // 2
N_ODD = DEPTH // 2
D_FF = 2816

POOL_WINDOWS = (2, 4, 8, 16)
POOL_GROUPS = len(POOL_WINDOWS)
POOL_DIM = D_MODEL // 2
POOL_GROUP_DIM = POOL_DIM // POOL_GROUPS

MLA_HEADS = 8
QK_NOPE_DIM = D_MODEL // 16
QK_ROPE_DIM = D_MODEL // 32
QK_HEAD_DIM = QK_NOPE_DIM + QK_ROPE_DIM
V_HEAD_DIM = D_MODEL // 16
Q_LORA_RANK = 3 * D_MODEL // 8
KV_LORA_RANK = D_MODEL // 4
ROPE_THETA = 10000.0
Q_BLOCK = 128

MIX_IN_EVEN = POOL_DIM + Q_LORA_RANK + KV_LORA_RANK + QK_ROPE_DIM
MIX_OUT_EVEN = POOL_DIM + MLA_HEADS * V_HEAD_DIM

CONV_DIM = D_MODEL
CONV_WIDTH = 3

NORM_EPS = 1e-6

kernel_name = "hybrid_pool_mla_shortconv_macaron"


def rms_norm(x, g):
    xf = x.astype(jnp.float32)
    y = xf * lax.rsqrt(jnp.mean(xf * xf, axis=-1, keepdims=True) + NORM_EPS)
    return (y * g.astype(jnp.float32)).astype(x.dtype)


def swiglu(x, w_gate, w_up, w_down):
    return (jax.nn.silu(x @ w_gate) * (x @ w_up)) @ w_down


def rotary(x, cos, sin):
    x1, x2 = jnp.split(x, 2, axis=-1)
    c = cos[None, :, None, :]
    s = sin[None, :, None, :]
    return jnp.concatenate([x1 * c - x2 * s, x2 * c + x1 * s], axis=-1)


def causal_multiscale_pool(u):
    s = u.shape[1]
    uf = u.astype(jnp.float32)
    cs = jnp.pad(jnp.cumsum(uf, axis=1), ((0, 0), (1, 0), (0, 0), (0, 0)))
    t = jnp.arange(1, s + 1)
    means = []
    for g, w in enumerate(POOL_WINDOWS):
        lo = jnp.maximum(t - w, 0)
        wsum = cs[:, 1:, g] - cs[:, lo, g]
        cnt = jnp.minimum(t, w).astype(jnp.float32)[None, :, None]
        means.append(wsum / cnt)
    mean = jnp.stack(means, axis=2)
    return (mean - uf).astype(u.dtype)


def causal_attention(q, k, v):
    b, s, h, dqk = q.shape
    dv = v.shape[-1]
    scale = dqk ** -0.5
    kpos = jnp.arange(s)

    def block(i):
        start = i * Q_BLOCK
        qb = lax.dynamic_slice_in_dim(q, start, Q_BLOCK, axis=1)
        sc = jnp.einsum('bqhd,bkhd->bhqk', qb, k,
                        preferred_element_type=jnp.float32) * scale
        qpos = start + jnp.arange(Q_BLOCK)
        sc = jnp.where(kpos[None, :] <= qpos[:, None], sc, -jnp.inf)
        p = jax.nn.softmax(sc, axis=-1)
        return jnp.einsum('bhqk,bkhd->bqhd', p.astype(v.dtype), v)

    out = lax.map(block, jnp.arange(s // Q_BLOCK))
    return jnp.moveaxis(out, 0, 1).reshape(b, s, h, dv)


def pool_mla_mixer(hn, cos, sin, w_in, q_a_norm, w_q_up, kv_a_norm, w_kv_up,
                   q_head_norm, k_head_norm, w_pool, pool_scale, w_out):
    b, s, _ = hn.shape
    z = hn @ w_in
    c1 = POOL_DIM
    c2 = c1 + Q_LORA_RANK
    c3 = c2 + KV_LORA_RANK
    u, q_lat, kv_lat, k_rope = jnp.split(z, [c1, c2, c3], axis=-1)

    u = u.reshape(b, s, POOL_GROUPS, POOL_GROUP_DIM)
    pooled = causal_multiscale_pool(u)
    pool_out = jnp.einsum('bsgc,gcd->bsgd', pooled, w_pool).reshape(b, s, POOL_DIM) * pool_scale

    q = (rms_norm(q_lat, q_a_norm) @ w_q_up).reshape(b, s, MLA_HEADS, QK_HEAD_DIM)
    kv = (rms_norm(kv_lat, kv_a_norm) @ w_kv_up).reshape(b, s, MLA_HEADS, QK_NOPE_DIM + V_HEAD_DIM)
    k_nope, v = jnp.split(kv, [QK_NOPE_DIM], axis=-1)
    k_rope_h = jnp.broadcast_to(k_rope[:, :, None, :], (b, s, MLA_HEADS, QK_ROPE_DIM))
    k = jnp.concatenate([k_nope, k_rope_h], axis=-1)
    q = rms_norm(q, q_head_norm)
    k = rms_norm(k, k_head_norm)
    q = jnp.concatenate([q[..., :QK_NOPE_DIM], rotary(q[..., QK_NOPE_DIM:], cos, sin)], axis=-1)
    k = jnp.concatenate([k[..., :QK_NOPE_DIM], rotary(k[..., QK_NOPE_DIM:], cos, sin)], axis=-1)
    attn = causal_attention(q, k, v).reshape(b, s, MLA_HEADS * V_HEAD_DIM)

    return jnp.concatenate([pool_out, attn], axis=-1) @ w_out


def gated_conv_mixer(hn, w_in, conv_w, w_out):
    gb, gc, hh = jnp.split(hn @ w_in, 3, axis=-1)
    u = gc * hh
    s = u.shape[1]
    up = jnp.pad(u, ((0, 0), (CONV_WIDTH - 1, 0), (0, 0)))
    y = conv_w[0] * up[:, 0:s]
    for j in range(1, CONV_WIDTH):
        y = y + conv_w[j] * up[:, j:j + s]
    return (gb * y) @ w_out


def setup_inputs(seed: int = 0) -> dict:
    key = jax.random.key(seed)
    ks = iter(jax.random.split(key, 32))

    def dense(shape, fan_in):
        return jax.random.normal(next(ks), shape, jnp.float32) * (fan_in ** -0.5)

    def gain(shape):
        return 1.0 + 0.05 * jax.random.normal(next(ks), shape, jnp.float32)

    return {
        "x": jax.random.normal(next(ks), (BATCH, SEQ, D_MODEL), jnp.float32),
        "ffn1_norm": gain((DEPTH, D_MODEL)),
        "ffn1_w_gate": dense((DEPTH, D_MODEL, D_FF), D_MODEL),
        "ffn1_w_up": dense((DEPTH, D_MODEL, D_FF), D_MODEL),
        "ffn1_w_down": dense((DEPTH, D_FF, D_MODEL), D_FF),
        "mix_norm": gain((DEPTH, D_MODEL)),
        "ffn2_norm": gain((DEPTH, D_MODEL)),
        "ffn2_w_gate": dense((DEPTH, D_MODEL, D_FF), D_MODEL),
        "ffn2_w_up": dense((DEPTH, D_MODEL, D_FF), D_MODEL),
        "ffn2_w_down": dense((DEPTH, D_FF, D_MODEL), D_FF),
        "a_w_in": dense((N_EVEN, D_MODEL, MIX_IN_EVEN), D_MODEL),
        "a_q_a_norm": gain((N_EVEN, Q_LORA_RANK)),
        "a_w_q_up": dense((N_EVEN, Q_LORA_RANK, MLA_HEADS * QK_HEAD_DIM), Q_LORA_RANK),
        "a_kv_a_norm": gain((N_EVEN, KV_LORA_RANK)),
        "a_w_kv_up": dense((N_EVEN, KV_LORA_RANK, MLA_HEADS * (QK_NOPE_DIM + V_HEAD_DIM)), KV_LORA_RANK),
        "a_q_head_norm": gain((N_EVEN, QK_HEAD_DIM)),
        "a_k_head_norm": gain((N_EVEN, QK_HEAD_DIM)),
        "a_w_pool": dense((N_EVEN, POOL_GROUPS, POOL_GROUP_DIM, POOL_GROUP_DIM), POOL_GROUP_DIM),
        "a_pool_scale": gain((N_EVEN, POOL_DIM)),
        "a_w_out": dense((N_EVEN, MIX_OUT_EVEN, D_MODEL), MIX_OUT_EVEN),
        "c_w_in": dense((N_ODD, D_MODEL, 3 * CONV_DIM), D_MODEL),
        "c_conv_w": dense((N_ODD, CONV_WIDTH, CONV_DIM), CONV_WIDTH),
        "c_w_out": dense((N_ODD, CONV_DIM, D_MODEL), CONV_DIM),
    }


def reference(x, ffn1_norm, ffn1_w_gate, ffn1_w_up, ffn1_w_down, mix_norm,
              ffn2_norm, ffn2_w_gate, ffn2_w_up, ffn2_w_down,
              a_w_in, a_q_a_norm, a_w_q_up, a_kv_a_norm, a_w_kv_up,
              a_q_head_norm, a_k_head_norm, a_w_pool, a_pool_scale, a_w_out,
              c_w_in, c_conv_w, c_w_out):
    s = x.shape[1]
    pos = jnp.arange(s, dtype=jnp.float32)
    inv_freq = ROPE_THETA ** (-jnp.arange(0, QK_ROPE_DIM, 2, dtype=jnp.float32) / QK_ROPE_DIM)
    ang = pos[:, None] * inv_freq[None, :]
    cos = jnp.cos(ang).astype(x.dtype)
    sin = jnp.sin(ang).astype(x.dtype)

    for layer in range(DEPTH):
        x = x + 0.5 * swiglu(rms_norm(x, ffn1_norm[layer]),
                             ffn1_w_gate[layer], ffn1_w_up[layer], ffn1_w_down[layer])
        hn = rms_norm(x, mix_norm[layer])
        i = layer // 2
        if layer % 2 == 0:
            x = x + pool_mla_mixer(hn, cos, sin, a_w_in[i], a_q_a_norm[i], a_w_q_up[i],
                                   a_kv_a_norm[i], a_w_kv_up[i], a_q_head_norm[i],
                                   a_k_head_norm[i], a_w_pool[i], a_pool_scale[i], a_w_out[i])
        else:
            x = x + gated_conv_mixer(hn, c_w_in[i], c_conv_w[i], c_w_out[i])
        x = x + 0.5 * swiglu(rms_norm(x, ffn2_norm[layer]),
                             ffn2_w_gate[layer], ffn2_w_up[layer], ffn2_w_down[layer])
    return x
```

```python
import functools

import jax
import jax.numpy as jnp
from jax import lax
from jax.experimental import pallas as pl
from jax.experimental.pallas import tpu as pltpu

D_MODEL = 1024
DEPTH = 4
D_FF = 2816
POOL_WINDOWS = (2, 4, 8, 16)
POOL_DIM = 512
POOL_GROUP_DIM = 128
MLA_HEADS = 8
QK_NOPE_DIM = 64
QK_ROPE_DIM = 32
QK_HEAD_DIM = 96
V_HEAD_DIM = 64
Q_LORA_RANK = 384
KV_LORA_RANK = 256
ROPE_THETA = 10000.0
CONV_WIDTH = 3
NORM_EPS = 1e-6

LANES = 128
HEAD_PAD = LANES
MIX_IN_PAD = POOL_DIM + Q_LORA_RANK + KV_LORA_RANK + HEAD_PAD
POOL_HALO = 16
CONV_HALO = 8
MASK_VALUE = -1e30
VMEM_LIMIT = 56 * 1024 * 1024

TOKEN_TILE = 512
ATTN_Q_TILE = 512
ATTN_K_TILE = 512

F32 = jnp.float32
BF16 = jnp.bfloat16


def _rms(x, g, n=None):
    n = x.shape[-1] if n is None else n
    ms = jnp.sum(x * x, axis=-1, keepdims=True) * (1.0 / n)
    return x * lax.rsqrt(ms + NORM_EPS) * g


def _const_spec(shape):
    nd = len(shape)
    return pl.BlockSpec(shape, lambda *_: (0,) * nd, pipeline_mode=pl.Buffered(1))


def _params(sem):
    return pltpu.CompilerParams(dimension_semantics=sem, vmem_limit_bytes=VMEM_LIMIT)


def _ffn_body(x_ref, g_ref, wg_ref, wu_ref, wd_ref, o_ref):
    x = x_ref[...]
    xn = _rms(x, g_ref[...]).astype(BF16)
    gate = jnp.dot(xn, wg_ref[...], preferred_element_type=F32)
    up = jnp.dot(xn, wu_ref[...], preferred_element_type=F32)
    h = (gate * jax.nn.sigmoid(gate) * up).astype(BF16)
    y = jnp.dot(h, wd_ref[...], preferred_element_type=F32)
    o_ref[...] = x + 0.5 * y


def _ffn(x, g, wg, wu, wd):
    s, d = x.shape
    tm = TOKEN_TILE
    return pl.pallas_call(
        _ffn_body,
        grid=(s // tm,),
        in_specs=[
            pl.BlockSpec((tm, d), lambda i: (i, 0)),
            _const_spec((1, d)),
            _const_spec(wg.shape),
            _const_spec(wu.shape),
            _const_spec(wd.shape),
        ],
        out_specs=pl.BlockSpec((tm, d), lambda i: (i, 0)),
        out_shape=jax.ShapeDtypeStruct((s, d), F32),
        compiler_params=_params(("parallel",)),
        name="ffn",
    )(x, g.reshape(1, d), wg, wu, wd)


def _rotate(x, cos_t, sin_lo_t, sin_hi_t):
    from_hi = pltpu.roll(x, HEAD_PAD - QK_ROPE_DIM // 2, axis=1)
    from_lo = pltpu.roll(x, QK_ROPE_DIM // 2, axis=1)
    return x * cos_t + from_hi * sin_lo_t + from_lo * sin_hi_t


def _mla_pre_body(x_ref, g_ref, w_in_ref, qa_ref, wq_ref, kva_ref, wk_ref, wv_ref,
                  qh_ref, kh_ref, wp_ref, ps_ref, cos_ref, slo_ref, shi_ref,
                  pool_ref, q_ref, k_ref, vt_ref, tail_ref):
    i = pl.program_id(0)
    tm = x_ref.shape[0]

    @pl.when(i == 0)
    def _():
        tail_ref[...] = jnp.zeros_like(tail_ref)

    hn = _rms(x_ref[...], g_ref[...]).astype(BF16)
    z = jnp.dot(hn, w_in_ref[...], preferred_element_type=F32)
    c1 = POOL_DIM
    c2 = c1 + Q_LORA_RANK
    c3 = c2 + KV_LORA_RANK
    u = z[:, :c1]
    q_lat = z[:, c1:c2]
    kv_lat = z[:, c2:c3]
    k_rope = z[:, c3:]

    ue = jnp.concatenate([tail_ref[...], u], axis=0)
    tail_ref[...] = u[tm - POOL_HALO:, :]
    pos = i * tm + lax.broadcasted_iota(jnp.int32, (tm, 1), 0)
    pooled_out = []
    for g, w in enumerate(POOL_WINDOWS):
        a = ue[:, g * POOL_GROUP_DIM:(g + 1) * POOL_GROUP_DIM]
        shift = 1
        while shift < w:
            a = a + pltpu.roll(a, shift, axis=0)
            shift *= 2
        wsum = a[POOL_HALO:, :]
        cnt = jnp.minimum(pos + 1, w).astype(F32)
        pooled = wsum / cnt - u[:, g * POOL_GROUP_DIM:(g + 1) * POOL_GROUP_DIM]
        pooled_out.append(jnp.dot(pooled.astype(BF16), wp_ref[g], preferred_element_type=F32))
    pool_ref[...] = (jnp.concatenate(pooled_out, axis=-1) * ps_ref[...]).astype(BF16)

    cos_t = cos_ref[...]
    slo_t = slo_ref[...]
    shi_t = shi_ref[...]

    qn = _rms(q_lat, qa_ref[...]).astype(BF16)
    q = jnp.dot(qn, wq_ref[...], preferred_element_type=F32)
    scale = QK_HEAD_DIM ** -0.5
    for h in range(MLA_HEADS):
        qh = _rms(q[:, h * HEAD_PAD:(h + 1) * HEAD_PAD], qh_ref[...], QK_HEAD_DIM)
        q_ref[h] = (_rotate(qh, cos_t, slo_t, shi_t) * scale).astype(BF16)

    kvn = _rms(kv_lat, kva_ref[...]).astype(BF16)
    kk = jnp.dot(kvn, wk_ref[...], preferred_element_type=F32)
    vv = jnp.dot(kvn, wv_ref[...], preferred_element_type=F32)
    ones_lane = (lax.broadcasted_iota(jnp.int32, (1, HEAD_PAD), 1) == V_HEAD_DIM).astype(F32)
    for h in range(MLA_HEADS):
        kh = _rms(kk[:, h * HEAD_PAD:(h + 1) * HEAD_PAD] + k_rope, kh_ref[...], QK_HEAD_DIM)
        k_ref[h] = _rotate(kh, cos_t, slo_t, shi_t).astype(BF16)
        vh = vv[:, h * HEAD_PAD:(h + 1) * HEAD_PAD] + ones_lane
        vt_ref[h] = vh.T.astype(BF16)


def _mla_pre(x, g, w_in, qa, wq, kva, wk, wv, qh, kh, wp, ps, cos_t, slo_t, shi_t):
    s, d = x.shape
    tm = TOKEN_TILE
    row = lambda i: (i, 0)
    return pl.pallas_call(
        _mla_pre_body,
        grid=(s // tm,),
        in_specs=[
            pl.BlockSpec((tm, d), row),
            _const_spec((1, d)),
            _const_spec(w_in.shape),
            _const_spec((1, Q_LORA_RANK)),
            _const_spec(wq.shape),
            _const_spec((1, KV_LORA_RANK)),
            _const_spec(wk.shape),
            _const_spec(wv.shape),
            _const_spec((1, HEAD_PAD)),
            _const_spec((1, HEAD_PAD)),
            _const_spec(wp.shape),
            _const_spec((1, POOL_DIM)),
            pl.BlockSpec((tm, HEAD_PAD), row),
            pl.BlockSpec((tm, HEAD_PAD), row),
            pl.BlockSpec((tm, HEAD_PAD), row),
        ],
        out_specs=[
            pl.BlockSpec((tm, POOL_DIM), row),
            pl.BlockSpec((MLA_HEADS, tm, HEAD_PAD), lambda i: (0, i, 0)),
            pl.BlockSpec((MLA_HEADS, tm, HEAD_PAD), lambda i: (0, i, 0)),
            pl.BlockSpec((MLA_HEADS, HEAD_PAD, tm), lambda i: (0, 0, i)),
        ],
        out_shape=[
            jax.ShapeDtypeStruct((s, POOL_DIM), BF16),
            jax.ShapeDtypeStruct((MLA_HEADS, s, HEAD_PAD), BF16),
            jax.ShapeDtypeStruct((MLA_HEADS, s, HEAD_PAD), BF16),
            jax.ShapeDtypeStruct((MLA_HEADS, HEAD_PAD, s), BF16),
        ],
        scratch_shapes=[pltpu.VMEM((POOL_HALO, POOL_DIM), F32)],
        compiler_params=_params(("arbitrary",)),
        name="mla_pre",
    )(x, g.reshape(1, d), w_in, qa.reshape(1, -1), wq, kva.reshape(1, -1), wk, wv,
      qh, kh, wp, ps.reshape(1, -1), cos_t, slo_t, shi_t)


def _flash_body(q_ref, k_ref, vt_ref, o_ref, m_ref, acc_ref):
    i = pl.program_id(1)
    tq = q_ref.shape[0]
    tk = ATTN_K_TILE
    q = q_ref[...]
    m_ref[...] = jnp.full_like(m_ref, MASK_VALUE)
    acc_ref[...] = jnp.zeros_like(acc_ref)

    def step(j, masked):
        off = pl.multiple_of(j * tk, tk)
        k = k_ref[pl.ds(off, tk), :]
        st = lax.dot_general(k, q, (((1,), (1,)), ((), ())), preferred_element_type=F32)
        if masked:
            kpos = off + lax.broadcasted_iota(jnp.int32, (tk, tq), 0)
            qpos = i * tq + lax.broadcasted_iota(jnp.int32, (tk, tq), 1)
            st = jnp.where(kpos <= qpos, st, MASK_VALUE)
        m_old = m_ref[...]
        m_new = jnp.maximum(m_old, jnp.max(st, axis=0, keepdims=True))
        p = jnp.exp(st - m_new).astype(BF16)
        alpha = jnp.exp(m_old - m_new)
        vt = vt_ref[:, pl.ds(off, tk)]
        acc_ref[...] = acc_ref[...] * alpha + jnp.dot(vt, p, preferred_element_type=F32)
        m_ref[...] = m_new

    blocks_per_q = tq // tk

    def body(j, carry):
        step(j, False)
        return carry

    lax.fori_loop(0, i * blocks_per_q, body, 0)
    for d in range(blocks_per_q):
        step(i * blocks_per_q + d, True)

    acc = acc_ref[...]
    o_ref[...] = (acc[:V_HEAD_DIM] / acc[V_HEAD_DIM:V_HEAD_DIM + 1]).astype(BF16)


def _flash(q, k, vt):
    h, s, dp = q.shape
    tq = ATTN_Q_TILE
    return pl.pallas_call(
        _flash_body,
        grid=(h, s // tq),
        in_specs=[
            pl.BlockSpec((None, tq, dp), lambda hh, i: (hh, i, 0)),
            pl.BlockSpec((None, s, dp), lambda hh, i: (hh, 0, 0)),
            pl.BlockSpec((None, dp, s), lambda hh, i: (hh, 0, 0)),
        ],
        out_specs=pl.BlockSpec((V_HEAD_DIM, tq), lambda hh, i: (hh, i)),
        out_shape=jax.ShapeDtypeStruct((h * V_HEAD_DIM, s), BF16),
        scratch_shapes=[pltpu.VMEM((1, tq), F32), pltpu.VMEM((dp, tq), F32)],
        compiler_params=_params(("parallel", "arbitrary")),
        name="flash",
    )(q, k, vt)


def _mix_out_body(x_ref, pool_ref, at_ref, wo1_ref, wo2_ref, o_ref):
    y = jnp.dot(pool_ref[...], wo1_ref[...], preferred_element_type=F32)
    y = y + lax.dot_general(at_ref[...], wo2_ref[...], (((0,), (0,)), ((), ())),
                            preferred_element_type=F32)
    o_ref[...] = x_ref[...] + y


def _mix_out(x, pool, attn_t, wo1, wo2):
    s, d = x.shape
    tm = TOKEN_TILE
    row = lambda i: (i, 0)
    return pl.pallas_call(
        _mix_out_body,
        grid=(s // tm,),
        in_specs=[
            pl.BlockSpec((tm, d), row),
            pl.BlockSpec((tm, POOL_DIM), row),
            pl.BlockSpec((attn_t.shape[0], tm), lambda i: (0, i)),
            _const_spec(wo1.shape),
            _const_spec(wo2.shape),
        ],
        out_specs=pl.BlockSpec((tm, d), row),
        out_shape=jax.ShapeDtypeStruct((s, d), F32),
        compiler_params=_params(("parallel",)),
        name="mix_out",
    )(x, pool, attn_t, wo1, wo2)


def _conv_body(x_ref, g_ref, w_in_ref, cw_ref, w_out_ref, o_ref, tail_ref):
    i = pl.program_id(0)
    tm, d = x_ref.shape

    @pl.when(i == 0)
    def _():
        tail_ref[...] = jnp.zeros_like(tail_ref)

    x = x_ref[...]
    hn = _rms(x, g_ref[...]).astype(BF16)
    z = jnp.dot(hn, w_in_ref[...], preferred_element_type=F32)
    gb = z[:, :d]
    u = z[:, d:2 * d] * z[:, 2 * d:]
    ue = jnp.concatenate([tail_ref[...], u], axis=0)
    tail_ref[...] = u[tm - CONV_HALO:, :]
    cw = cw_ref[...]
    y = cw[CONV_WIDTH - 1:CONV_WIDTH] * u
    for back in range(1, CONV_WIDTH):
        tap = CONV_WIDTH - 1 - back
        y = y + cw[tap:tap + 1] * pltpu.roll(ue, back, axis=0)[CONV_HALO:, :]
    y = (gb * y).astype(BF16)
    o_ref[...] = x + jnp.dot(y, w_out_ref[...], preferred_element_type=F32)


def _conv_mixer(x, g, w_in, cw, w_out):
    s, d = x.shape
    tm = TOKEN_TILE
    row = lambda i: (i, 0)
    return pl.pallas_call(
        _conv_body,
        grid=(s // tm,),
        in_specs=[
            pl.BlockSpec((tm, d), row),
            _const_spec((1, d)),
            _const_spec(w_in.shape),
            _const_spec(cw.shape),
            _const_spec(w_out.shape),
        ],
        out_specs=pl.BlockSpec((tm, d), row),
        out_shape=jax.ShapeDtypeStruct((s, d), F32),
        scratch_shapes=[pltpu.VMEM((CONV_HALO, d), F32)],
        compiler_params=_params(("arbitrary",)),
        name="conv_mixer",
    )(x, g.reshape(1, d), w_in, cw, w_out)


def _pad_heads(w, head_dim, take_from, take_n):
    rows = w.shape[0]
    w = w.reshape(rows, MLA_HEADS, head_dim)[:, :, take_from:take_from + take_n]
    w = jnp.pad(w, ((0, 0), (0, 0), (0, HEAD_PAD - take_n)))
    return w.reshape(rows, MLA_HEADS * HEAD_PAD)


def _rope_tables(s):
    pos = jnp.arange(s, dtype=F32)
    inv_freq = ROPE_THETA ** (-jnp.arange(0, QK_ROPE_DIM, 2, dtype=F32) / QK_ROPE_DIM)
    ang = pos[:, None] * inv_freq[None, :]
    cos = jnp.cos(ang)
    sin = jnp.sin(ang)
    half = QK_ROPE_DIM // 2
    ones = jnp.ones((s, QK_NOPE_DIM), F32)
    zeros_nope = jnp.zeros((s, QK_NOPE_DIM), F32)
    zeros_half = jnp.zeros((s, half), F32)
    zeros_tail = jnp.zeros((s, HEAD_PAD - QK_HEAD_DIM), F32)
    cos_t = jnp.concatenate([ones, cos, cos, zeros_tail], axis=-1)
    sin_lo_t = jnp.concatenate([zeros_nope, -sin, zeros_half, zeros_tail], axis=-1)
    sin_hi_t = jnp.concatenate([zeros_nope, zeros_half, sin, zeros_tail], axis=-1)
    return cos_t, sin_lo_t, sin_hi_t


def kernel(x, ffn1_norm, ffn1_w_gate, ffn1_w_up, ffn1_w_down, mix_norm, ffn2_norm, ffn2_w_gate, ffn2_w_up, ffn2_w_down, a_w_in, a_q_a_norm, a_w_q_up, a_kv_a_norm, a_w_kv_up, a_q_head_norm, a_k_head_norm, a_w_pool, a_pool_scale, a_w_out, c_w_in, c_conv_w, c_w_out):
    b, s, d = x.shape
    assert b == 1 and d == D_MODEL and s % TOKEN_TILE == 0 and s % ATTN_Q_TILE == 0
    h = x.reshape(s, d)
    cos_t, sin_lo_t, sin_hi_t = _rope_tables(s)
    head_pad = ((0, 0), (0, HEAD_PAD - QK_HEAD_DIM))

    for layer in range(DEPTH):
        h = _ffn(h, ffn1_norm[layer], ffn1_w_gate[layer].astype(BF16),
                 ffn1_w_up[layer].astype(BF16), ffn1_w_down[layer].astype(BF16))
        i = layer // 2
        if layer % 2 == 0:
            w_in = a_w_in[i]
            c3 = POOL_DIM + Q_LORA_RANK + KV_LORA_RANK
            w_in_p = jnp.concatenate(
                [w_in[:, :c3], jnp.zeros((d, QK_NOPE_DIM), F32), w_in[:, c3:],
                 jnp.zeros((d, HEAD_PAD - QK_HEAD_DIM), F32)], axis=-1).astype(BF16)
            wq = _pad_heads(a_w_q_up[i], QK_HEAD_DIM, 0, QK_HEAD_DIM).astype(BF16)
            wk = _pad_heads(a_w_kv_up[i], QK_NOPE_DIM + V_HEAD_DIM, 0, QK_NOPE_DIM).astype(BF16)
            wv = _pad_heads(a_w_kv_up[i], QK_NOPE_DIM + V_HEAD_DIM, QK_NOPE_DIM, V_HEAD_DIM).astype(BF16)
            qh = jnp.pad(a_q_head_norm[i].reshape(1, -1), head_pad)
            kh = jnp.pad(a_k_head_norm[i].reshape(1, -1), head_pad)
            pool, q, k, vt = _mla_pre(h, mix_norm[layer], w_in_p, a_q_a_norm[i], wq, a_kv_a_norm[i],
                                      wk, wv, qh, kh, a_w_pool[i].astype(BF16), a_pool_scale[i],
                                      cos_t, sin_lo_t, sin_hi_t)
            attn_t = _flash(q, k, vt)
            w_out = a_w_out[i].astype(BF16)
            h = _mix_out(h, pool, attn_t, w_out[:POOL_DIM], w_out[POOL_DIM:])
        else:
            h = _conv_mixer(h, mix_norm[layer], c_w_in[i].astype(BF16), c_conv_w[i],
                            c_w_out[i].astype(BF16))
        h = _ffn(h, ffn2_norm[layer], ffn2_w_gate[layer].astype(BF16),
                 ffn2_w_up[layer].astype(BF16), ffn2_w_down[layer].astype(BF16))
    return h.reshape(b, s, d)
```

```python
import functools

import jax
import jax.numpy as jnp
from jax import lax
from jax.experimental import pallas as pl
from jax.experimental.pallas import tpu as pltpu

D_MODEL = 1024
DEPTH = 4
D_FF = 2816
POOL_WINDOWS = (2, 4, 8, 16)
POOL_DIM = 512
POOL_GROUP_DIM = 128
MLA_HEADS = 8
QK_NOPE_DIM = 64
QK_ROPE_DIM = 32
QK_HEAD_DIM = 96
V_HEAD_DIM = 64
Q_LORA_RANK = 384
KV_LORA_RANK = 256
ROPE_THETA = 10000.0
CONV_WIDTH = 3
NORM_EPS = 1e-6

LANES = 128
HEAD_PAD = LANES
MIX_IN_PAD = POOL_DIM + Q_LORA_RANK + KV_LORA_RANK + HEAD_PAD
POOL_HALO = 16
CONV_HALO = 8
MASK_VALUE = -1e30
VMEM_LIMIT = 56 * 1024 * 1024

V_ROWS = 80
LOG2_E = 1.4426950408889634

TOKEN_TILE = 512
ATTN_Q_TILE = 1024
ATTN_K_TILE = 512

F32 = jnp.float32
BF16 = jnp.bfloat16


def _rms(x, g, n=None):
    n = x.shape[-1] if n is None else n
    ms = jnp.sum(x * x, axis=-1, keepdims=True) * (1.0 / n)
    return x * lax.rsqrt(ms + NORM_EPS) * g


def _const_spec(shape):
    nd = len(shape)
    return pl.BlockSpec(shape, lambda *_: (0,) * nd, pipeline_mode=pl.Buffered(1))


def _params(sem):
    return pltpu.CompilerParams(dimension_semantics=sem, vmem_limit_bytes=VMEM_LIMIT)


def _ffn_body(x_ref, g_ref, wg_ref, wu_ref, wd_ref, o_ref):
    x = x_ref[...]
    xn = _rms(x, g_ref[...]).astype(BF16)
    gate = jnp.dot(xn, wg_ref[...], preferred_element_type=F32)
    up = jnp.dot(xn, wu_ref[...], preferred_element_type=F32)
    h = (gate * jax.nn.sigmoid(gate) * up).astype(BF16)
    y = jnp.dot(h, wd_ref[...], preferred_element_type=F32)
    o_ref[...] = x + 0.5 * y


def _ffn(x, g, wg, wu, wd):
    s, d = x.shape
    tm = TOKEN_TILE
    return pl.pallas_call(
        _ffn_body,
        grid=(s // tm,),
        in_specs=[
            pl.BlockSpec((tm, d), lambda i: (i, 0)),
            _const_spec((1, d)),
            _const_spec(wg.shape),
            _const_spec(wu.shape),
            _const_spec(wd.shape),
        ],
        out_specs=pl.BlockSpec((tm, d), lambda i: (i, 0)),
        out_shape=jax.ShapeDtypeStruct((s, d), F32),
        compiler_params=_params(("parallel",)),
        name="ffn",
    )(x, g.reshape(1, d), wg, wu, wd)


def _rotate(x, cos_t, sin_lo_t, sin_hi_t):
    from_hi = pltpu.roll(x, HEAD_PAD - QK_ROPE_DIM // 2, axis=1)
    from_lo = pltpu.roll(x, QK_ROPE_DIM // 2, axis=1)
    return x * cos_t + from_hi * sin_lo_t + from_lo * sin_hi_t


def _mla_pre_body(x_ref, g_ref, w_in_ref, qa_ref, wq_ref, kva_ref, wk_ref, wv_ref,
                  qh_ref, kh_ref, wp_ref, ps_ref, cos_ref, slo_ref, shi_ref,
                  pool_ref, qt_ref, k_ref, vt_ref, tail_ref):
    i = pl.program_id(0)
    tm = x_ref.shape[0]

    @pl.when(i == 0)
    def _():
        tail_ref[...] = jnp.zeros_like(tail_ref)

    hn = _rms(x_ref[...], g_ref[...]).astype(BF16)
    z = jnp.dot(hn, w_in_ref[...], preferred_element_type=F32)
    c1 = POOL_DIM
    c2 = c1 + Q_LORA_RANK
    c3 = c2 + KV_LORA_RANK
    u = z[:, :c1]
    q_lat = z[:, c1:c2]
    kv_lat = z[:, c2:c3]
    k_rope = z[:, c3:]

    ue = jnp.concatenate([tail_ref[...], u], axis=0)
    tail_ref[...] = u[tm - POOL_HALO:, :]
    pos = i * tm + lax.broadcasted_iota(jnp.int32, (tm, 1), 0)
    pooled_out = []
    for g, w in enumerate(POOL_WINDOWS):
        a = ue[:, g * POOL_GROUP_DIM:(g + 1) * POOL_GROUP_DIM]
        shift = 1
        while shift < w:
            a = a + pltpu.roll(a, shift, axis=0)
            shift *= 2
        wsum = a[POOL_HALO:, :]
        cnt = jnp.minimum(pos + 1, w).astype(F32)
        pooled = wsum / cnt - u[:, g * POOL_GROUP_DIM:(g + 1) * POOL_GROUP_DIM]
        pooled_out.append(jnp.dot(pooled.astype(BF16), wp_ref[g], preferred_element_type=F32))
    pool_ref[...] = (jnp.concatenate(pooled_out, axis=-1) * ps_ref[...]).astype(BF16)

    cos_t = cos_ref[...]
    slo_t = slo_ref[...]
    shi_t = shi_ref[...]

    qn = _rms(q_lat, qa_ref[...]).astype(BF16)
    q = jnp.dot(qn, wq_ref[...], preferred_element_type=F32)
    scale = QK_HEAD_DIM ** -0.5 * LOG2_E
    for h in range(MLA_HEADS):
        qh = _rms(q[:, h * HEAD_PAD:(h + 1) * HEAD_PAD], qh_ref[...], QK_HEAD_DIM)
        qt_ref[h] = (_rotate(qh, cos_t, slo_t, shi_t) * scale).T.astype(BF16)

    kvn = _rms(kv_lat, kva_ref[...]).astype(BF16)
    kk = jnp.dot(kvn, wk_ref[...], preferred_element_type=F32)
    vv = jnp.dot(kvn, wv_ref[...], preferred_element_type=F32)
    ones_lane = (lax.broadcasted_iota(jnp.int32, (1, HEAD_PAD), 1) == V_HEAD_DIM).astype(F32)
    for h in range(MLA_HEADS):
        kh = _rms(kk[:, h * HEAD_PAD:(h + 1) * HEAD_PAD] + k_rope, kh_ref[...], QK_HEAD_DIM)
        k_ref[h] = _rotate(kh, cos_t, slo_t, shi_t).astype(BF16)
        vh = vv[:, h * HEAD_PAD:(h + 1) * HEAD_PAD] + ones_lane
        vt_ref[h] = vh.T[:V_ROWS].astype(BF16)


def _mla_pre(x, g, w_in, qa, wq, kva, wk, wv, qh, kh, wp, ps, cos_t, slo_t, shi_t):
    s, d = x.shape
    tm = TOKEN_TILE
    row = lambda i: (i, 0)
    return pl.pallas_call(
        _mla_pre_body,
        grid=(s // tm,),
        in_specs=[
            pl.BlockSpec((tm, d), row),
            _const_spec((1, d)),
            _const_spec(w_in.shape),
            _const_spec((1, Q_LORA_RANK)),
            _const_spec(wq.shape),
            _const_spec((1, KV_LORA_RANK)),
            _const_spec(wk.shape),
            _const_spec(wv.shape),
            _const_spec((1, HEAD_PAD)),
            _const_spec((1, HEAD_PAD)),
            _const_spec(wp.shape),
            _const_spec((1, POOL_DIM)),
            pl.BlockSpec((tm, HEAD_PAD), row),
            pl.BlockSpec((tm, HEAD_PAD), row),
            pl.BlockSpec((tm, HEAD_PAD), row),
        ],
        out_specs=[
            pl.BlockSpec((tm, POOL_DIM), row),
            pl.BlockSpec((MLA_HEADS, HEAD_PAD, tm), lambda i: (0, 0, i)),
            pl.BlockSpec((MLA_HEADS, tm, HEAD_PAD), lambda i: (0, i, 0)),
            pl.BlockSpec((MLA_HEADS, V_ROWS, tm), lambda i: (0, 0, i)),
        ],
        out_shape=[
            jax.ShapeDtypeStruct((s, POOL_DIM), BF16),
            jax.ShapeDtypeStruct((MLA_HEADS, HEAD_PAD, s), BF16),
            jax.ShapeDtypeStruct((MLA_HEADS, s, HEAD_PAD), BF16),
            jax.ShapeDtypeStruct((MLA_HEADS, V_ROWS, s), BF16),
        ],
        scratch_shapes=[pltpu.VMEM((POOL_HALO, POOL_DIM), F32)],
        compiler_params=_params(("arbitrary",)),
        name="mla_pre",
    )(x, g.reshape(1, d), w_in, qa.reshape(1, -1), wq, kva.reshape(1, -1), wk, wv,
      qh, kh, wp, ps.reshape(1, -1), cos_t, slo_t, shi_t)


def _flash_body(qt_ref, k_ref, vt_ref, o_ref, sa_ref, sb_ref, m_ref, acc_ref):
    i = pl.program_id(1)
    tq = qt_ref.shape[1]
    tk = ATTN_K_TILE
    qt = qt_ref[...]
    m_ref[...] = jnp.full_like(m_ref, MASK_VALUE)
    acc_ref[...] = jnp.zeros_like(acc_ref)

    def scores(j, dst_ref):
        off = pl.multiple_of(j * tk, tk)
        dst_ref[...] = jnp.dot(k_ref[pl.ds(off, tk), :], qt, preferred_element_type=F32)

    def consume(j, src_ref, masked):
        off = pl.multiple_of(j * tk, tk)
        st = src_ref[...]
        if masked:
            kpos = off + lax.broadcasted_iota(jnp.int32, (tk, tq), 0)
            qpos = i * tq + lax.broadcasted_iota(jnp.int32, (tk, tq), 1)
            st = jnp.where(kpos <= qpos, st, MASK_VALUE)
        m_old = m_ref[...]
        m_new = jnp.maximum(m_old, jnp.max(st, axis=0, keepdims=True))
        p = jnp.exp2(st - m_new).astype(BF16)
        alpha = jnp.exp2(m_old - m_new)
        vt = vt_ref[:, pl.ds(off, tk)]
        acc_ref[...] = acc_ref[...] * alpha + jnp.dot(vt, p, preferred_element_type=F32)
        m_ref[...] = m_new

    scores(0, sa_ref)

    def pair(jj, carry):
        j = 2 * jj
        scores(j + 1, sb_ref)
        consume(j, sa_ref, False)
        scores(j + 2, sa_ref)
        consume(j + 1, sb_ref, False)
        return carry

    lax.fori_loop(0, i, pair, 0)
    scores(2 * i + 1, sb_ref)
    consume(2 * i, sa_ref, True)
    consume(2 * i + 1, sb_ref, True)

    acc = acc_ref[...]
    o_ref[...] = (acc[:V_HEAD_DIM] / acc[V_HEAD_DIM:V_HEAD_DIM + 1]).astype(BF16)


def _flash(qt, k, vt):
    h, s, dp = k.shape
    tq, tk = ATTN_Q_TILE, ATTN_K_TILE
    assert tq == 2 * tk
    return pl.pallas_call(
        _flash_body,
        grid=(h, s // tq),
        in_specs=[
            pl.BlockSpec((None, dp, tq), lambda hh, i: (hh, 0, i)),
            pl.BlockSpec((None, s, dp), lambda hh, i: (hh, 0, 0)),
            pl.BlockSpec((None, V_ROWS, s), lambda hh, i: (hh, 0, 0)),
        ],
        out_specs=pl.BlockSpec((V_HEAD_DIM, tq), lambda hh, i: (hh, i)),
        out_shape=jax.ShapeDtypeStruct((h * V_HEAD_DIM, s), BF16),
        scratch_shapes=[pltpu.VMEM((tk, tq), F32), pltpu.VMEM((tk, tq), F32),
                        pltpu.VMEM((1, tq), F32), pltpu.VMEM((V_ROWS, tq), F32)],
        compiler_params=_params(("parallel", "arbitrary")),
        name="flash",
    )(qt, k, vt)


def _mix_out_body(x_ref, pool_ref, at_ref, wo1_ref, wo2_ref, o_ref):
    y = jnp.dot(pool_ref[...], wo1_ref[...], preferred_element_type=F32)
    y = y + lax.dot_general(at_ref[...], wo2_ref[...], (((0,), (0,)), ((), ())),
                            preferred_element_type=F32)
    o_ref[...] = x_ref[...] + y


def _mix_out(x, pool, attn_t, wo1, wo2):
    s, d = x.shape
    tm = TOKEN_TILE
    row = lambda i: (i, 0)
    return pl.pallas_call(
        _mix_out_body,
        grid=(s // tm,),
        in_specs=[
            pl.BlockSpec((tm, d), row),
            pl.BlockSpec((tm, POOL_DIM), row),
            pl.BlockSpec((attn_t.shape[0], tm), lambda i: (0, i)),
            _const_spec(wo1.shape),
            _const_spec(wo2.shape),
        ],
        out_specs=pl.BlockSpec((tm, d), row),
        out_shape=jax.ShapeDtypeStruct((s, d), F32),
        compiler_params=_params(("parallel",)),
        name="mix_out",
    )(x, pool, attn_t, wo1, wo2)


def _conv_body(x_ref, g_ref, w_in_ref, cw_ref, w_out_ref, o_ref, tail_ref):
    i = pl.program_id(0)
    tm, d = x_ref.shape

    @pl.when(i == 0)
    def _():
        tail_ref[...] = jnp.zeros_like(tail_ref)

    x = x_ref[...]
    hn = _rms(x, g_ref[...]).astype(BF16)
    z = jnp.dot(hn, w_in_ref[...], preferred_element_type=F32)
    gb = z[:, :d]
    u = z[:, d:2 * d] * z[:, 2 * d:]
    ue = jnp.concatenate([tail_ref[...], u], axis=0)
    tail_ref[...] = u[tm - CONV_HALO:, :]
    cw = cw_ref[...]
    y = cw[CONV_WIDTH - 1:CONV_WIDTH] * u
    for back in range(1, CONV_WIDTH):
        tap = CONV_WIDTH - 1 - back
        y = y + cw[tap:tap + 1] * pltpu.roll(ue, back, axis=0)[CONV_HALO:, :]
    y = (gb * y).astype(BF16)
    o_ref[...] = x + jnp.dot(y, w_out_ref[...], preferred_element_type=F32)


def _conv_mixer(x, g, w_in, cw, w_out):
    s, d = x.shape
    tm = TOKEN_TILE
    row = lambda i: (i, 0)
    return pl.pallas_call(
        _conv_body,
        grid=(s // tm,),
        in_specs=[
            pl.BlockSpec((tm, d), row),
            _const_spec((1, d)),
            _const_spec(w_in.shape),
            _const_spec(cw.shape),
            _const_spec(w_out.shape),
        ],
        out_specs=pl.BlockSpec((tm, d), row),
        out_shape=jax.ShapeDtypeStruct((s, d), F32),
        scratch_shapes=[pltpu.VMEM((CONV_HALO, d), F32)],
        compiler_params=_params(("arbitrary",)),
        name="conv_mixer",
    )(x, g.reshape(1, d), w_in, cw, w_out)


def _pad_heads(w, head_dim, take_from, take_n):
    rows = w.shape[0]
    w = w.reshape(rows, MLA_HEADS, head_dim)[:, :, take_from:take_from + take_n]
    w = jnp.pad(w, ((0, 0), (0, 0), (0, HEAD_PAD - take_n)))
    return w.reshape(rows, MLA_HEADS * HEAD_PAD)


def _rope_tables(s):
    pos = jnp.arange(s, dtype=F32)
    inv_freq = ROPE_THETA ** (-jnp.arange(0, QK_ROPE_DIM, 2, dtype=F32) / QK_ROPE_DIM)
    ang = pos[:, None] * inv_freq[None, :]
    cos = jnp.cos(ang)
    sin = jnp.sin(ang)
    half = QK_ROPE_DIM // 2
    ones = jnp.ones((s, QK_NOPE_DIM), F32)
    zeros_nope = jnp.zeros((s, QK_NOPE_DIM), F32)
    zeros_half = jnp.zeros((s, half), F32)
    zeros_tail = jnp.zeros((s, HEAD_PAD - QK_HEAD_DIM), F32)
    cos_t = jnp.concatenate([ones, cos, cos, zeros_tail], axis=-1)
    sin_lo_t = jnp.concatenate([zeros_nope, -sin, zeros_half, zeros_tail], axis=-1)
    sin_hi_t = jnp.concatenate([zeros_nope, zeros_half, sin, zeros_tail], axis=-1)
    return cos_t, sin_lo_t, sin_hi_t


def kernel(x, ffn1_norm, ffn1_w_gate, ffn1_w_up, ffn1_w_down, mix_norm, ffn2_norm, ffn2_w_gate, ffn2_w_up, ffn2_w_down, a_w_in, a_q_a_norm, a_w_q_up, a_kv_a_norm, a_w_kv_up, a_q_head_norm, a_k_head_norm, a_w_pool, a_pool_scale, a_w_out, c_w_in, c_conv_w, c_w_out):
    b, s, d = x.shape
    assert b == 1 and d == D_MODEL and s % TOKEN_TILE == 0 and s % ATTN_Q_TILE == 0
    h = x.reshape(s, d)
    cos_t, sin_lo_t, sin_hi_t = _rope_tables(s)
    head_pad = ((0, 0), (0, HEAD_PAD - QK_HEAD_DIM))

    for layer in range(DEPTH):
        h = _ffn(h, ffn1_norm[layer], ffn1_w_gate[layer].astype(BF16),
                 ffn1_w_up[layer].astype(BF16), ffn1_w_down[layer].astype(BF16))
        i = layer // 2
        if layer % 2 == 0:
            w_in = a_w_in[i]
            c3 = POOL_DIM + Q_LORA_RANK + KV_LORA_RANK
            w_in_p = jnp.concatenate(
                [w_in[:, :c3], jnp.zeros((d, QK_NOPE_DIM), F32), w_in[:, c3:],
                 jnp.zeros((d, HEAD_PAD - QK_HEAD_DIM), F32)], axis=-1).astype(BF16)
            wq = _pad_heads(a_w_q_up[i], QK_HEAD_DIM, 0, QK_HEAD_DIM).astype(BF16)
            wk = _pad_heads(a_w_kv_up[i], QK_NOPE_DIM + V_HEAD_DIM, 0, QK_NOPE_DIM).astype(BF16)
            wv = _pad_heads(a_w_kv_up[i], QK_NOPE_DIM + V_HEAD_DIM, QK_NOPE_DIM, V_HEAD_DIM).astype(BF16)
            qh = jnp.pad(a_q_head_norm[i].reshape(1, -1), head_pad)
            kh = jnp.pad(a_k_head_norm[i].reshape(1, -1), head_pad)
            pool, qt, k, vt = _mla_pre(h, mix_norm[layer], w_in_p, a_q_a_norm[i], wq, a_kv_a_norm[i],
                                      wk, wv, qh, kh, a_w_pool[i].astype(BF16), a_pool_scale[i],
                                      cos_t, sin_lo_t, sin_hi_t)
            attn_t = _flash(qt, k, vt)
            w_out = a_w_out[i].astype(BF16)
            h = _mix_out(h, pool, attn_t, w_out[:POOL_DIM], w_out[POOL_DIM:])
        else:
            h = _conv_mixer(h, mix_norm[layer], c_w_in[i].astype(BF16), c_conv_w[i],
                            c_w_out[i].astype(BF16))
        h = _ffn(h, ffn2_norm[layer], ffn2_w_gate[layer].astype(BF16),
                 ffn2_w_up[layer].astype(BF16), ffn2_w_down[layer].astype(BF16))
    return h.reshape(b, s, d)
```

```python
import functools

import jax
import jax.numpy as jnp
from jax import lax
from jax.experimental import pallas as pl
from jax.experimental.pallas import tpu as pltpu

D_MODEL = 1024
DEPTH = 4
D_FF = 2816
POOL_WINDOWS = (2, 4, 8, 16)
POOL_DIM = 512
POOL_GROUP_DIM = 128
MLA_HEADS = 8
QK_NOPE_DIM = 64
QK_ROPE_DIM = 32
QK_HEAD_DIM = 96
V_HEAD_DIM = 64
Q_LORA_RANK = 384
KV_LORA_RANK = 256
ROPE_THETA = 10000.0
CONV_WIDTH = 3
NORM_EPS = 1e-6

LANES = 128
HEAD_PAD = LANES
MIX_IN_PAD = POOL_DIM + Q_LORA_RANK + KV_LORA_RANK + HEAD_PAD
POOL_HALO = 16
CONV_HALO = 8
MASK_VALUE = -1e30
VMEM_LIMIT = 56 * 1024 * 1024

V_ROWS = 80
LOG2_E = 1.4426950408889634

TOKEN_TILE = 512
FFN_TILE = 256
ATTN_Q_TILE = 1024
ATTN_K_TILE = 256
ATTN_BLOCKS_PER_Q = ATTN_Q_TILE // ATTN_K_TILE

F32 = jnp.float32
BF16 = jnp.bfloat16


def _rms(x, g, n=None):
    n = x.shape[-1] if n is None else n
    ms = jnp.sum(x * x, axis=-1, keepdims=True) * (1.0 / n)
    return x * lax.rsqrt(ms + NORM_EPS) * g


def _const_spec(shape):
    nd = len(shape)
    return pl.BlockSpec(shape, lambda *_: (0,) * nd, pipeline_mode=pl.Buffered(1))


def _layer_spec(stacked, layer):
    nd = stacked.ndim
    return pl.BlockSpec((None,) + stacked.shape[1:], lambda *_: (layer,) + (0,) * (nd - 1),
                        pipeline_mode=pl.Buffered(1))


def _rows(v):
    return v.reshape(v.shape[0], 1, v.shape[1])


def _params(sem):
    return pltpu.CompilerParams(dimension_semantics=sem, vmem_limit_bytes=VMEM_LIMIT)


def _ffn_body(x_ref, g_ref, wg_ref, wu_ref, wd_ref, o_ref):
    x = x_ref[...]
    xn = _rms(x, g_ref[...]).astype(wg_ref.dtype)
    gate = jnp.dot(xn, wg_ref[...], preferred_element_type=F32)
    up = jnp.dot(xn, wu_ref[...], preferred_element_type=F32)
    h = (gate * jax.nn.sigmoid(gate) * up).astype(wd_ref.dtype)
    y = jnp.dot(h, wd_ref[...], preferred_element_type=F32)
    o_ref[...] = x + 0.5 * y


def _ffn(x, layer, g, wg, wu, wd):
    s, d = x.shape
    tm = FFN_TILE
    g = _rows(g)
    return pl.pallas_call(
        _ffn_body,
        grid=(s // tm,),
        in_specs=[
            pl.BlockSpec((tm, d), lambda i: (i, 0)),
            _layer_spec(g, layer),
            _layer_spec(wg, layer),
            _layer_spec(wu, layer),
            _layer_spec(wd, layer),
        ],
        out_specs=pl.BlockSpec((tm, d), lambda i: (i, 0)),
        out_shape=jax.ShapeDtypeStruct((s, d), F32),
        compiler_params=_params(("parallel",)),
        name="ffn",
    )(x, g, wg, wu, wd)


def _rotate(x, cos_t, sin_lo_t, sin_hi_t):
    from_hi = pltpu.roll(x, HEAD_PAD - QK_ROPE_DIM // 2, axis=1)
    from_lo = pltpu.roll(x, QK_ROPE_DIM // 2, axis=1)
    return x * cos_t + from_hi * sin_lo_t + from_lo * sin_hi_t


def _mla_pre_body(x_ref, g_ref, w_in_ref, qa_ref, wq_ref, kva_ref, wk_ref, wv_ref,
                  qh_ref, kh_ref, wp_ref, ps_ref, cos_ref, slo_ref, shi_ref,
                  pool_ref, qt_ref, k_ref, vt_ref, tail_ref):
    i = pl.program_id(0)
    tm = x_ref.shape[0]

    @pl.when(i == 0)
    def _():
        tail_ref[...] = jnp.zeros_like(tail_ref)

    hn = _rms(x_ref[...], g_ref[...]).astype(BF16)
    z = jnp.dot(hn, w_in_ref[...], preferred_element_type=F32)
    c1 = POOL_DIM
    c2 = c1 + Q_LORA_RANK
    c3 = c2 + KV_LORA_RANK
    u = z[:, :c1]
    q_lat = z[:, c1:c2]
    kv_lat = z[:, c2:c3]
    k_rope = z[:, c3:]

    ue = jnp.concatenate([tail_ref[...], u], axis=0)
    tail_ref[...] = u[tm - POOL_HALO:, :]
    pos = i * tm + lax.broadcasted_iota(jnp.int32, (tm, 1), 0)
    pooled_out = []
    for g, w in enumerate(POOL_WINDOWS):
        a = ue[:, g * POOL_GROUP_DIM:(g + 1) * POOL_GROUP_DIM]
        shift = 1
        while shift < w:
            a = a + pltpu.roll(a, shift, axis=0)
            shift *= 2
        wsum = a[POOL_HALO:, :]
        cnt = jnp.minimum(pos + 1, w).astype(F32)
        pooled = wsum / cnt - u[:, g * POOL_GROUP_DIM:(g + 1) * POOL_GROUP_DIM]
        pooled_out.append(jnp.dot(pooled.astype(BF16), wp_ref[g], preferred_element_type=F32))
    pool_ref[...] = (jnp.concatenate(pooled_out, axis=-1) * ps_ref[...]).astype(BF16)

    cos_t = cos_ref[...]
    slo_t = slo_ref[...]
    shi_t = shi_ref[...]

    qn = _rms(q_lat, qa_ref[...]).astype(BF16)
    q = jnp.dot(qn, wq_ref[...], preferred_element_type=F32)
    scale = QK_HEAD_DIM ** -0.5 * LOG2_E
    for h in range(MLA_HEADS):
        qh = _rms(q[:, h * HEAD_PAD:(h + 1) * HEAD_PAD], qh_ref[...], QK_HEAD_DIM)
        qt_ref[h] = (_rotate(qh, cos_t, slo_t, shi_t) * scale).T.astype(BF16)

    kvn = _rms(kv_lat, kva_ref[...]).astype(BF16)
    kk = jnp.dot(kvn, wk_ref[...], preferred_element_type=F32)
    vv = jnp.dot(kvn, wv_ref[...], preferred_element_type=F32)
    ones_lane = (lax.broadcasted_iota(jnp.int32, (1, HEAD_PAD), 1) == V_HEAD_DIM).astype(F32)
    gk = kh_ref[...]
    rope_rot = _rotate(k_rope * gk, cos_t, slo_t, shi_t)
    rope_ssq = jnp.sum(k_rope * k_rope, axis=-1, keepdims=True)
    for h in range(MLA_HEADS):
        kn = kk[:, h * HEAD_PAD:(h + 1) * HEAD_PAD]
        ms = (jnp.sum(kn * kn, axis=-1, keepdims=True) + rope_ssq) * (1.0 / QK_HEAD_DIM)
        k_ref[h] = ((kn * gk + rope_rot) * lax.rsqrt(ms + NORM_EPS)).astype(BF16)
        vh = vv[:, h * HEAD_PAD:(h + 1) * HEAD_PAD] + ones_lane
        vt_ref[h] = vh.T[:V_ROWS].astype(BF16)


def _mla_pre(x, g, w_in, qa, wq, kva, wk, wv, qh, kh, wp, ps, cos_t, slo_t, shi_t):
    s, d = x.shape
    tm = TOKEN_TILE
    row = lambda i: (i, 0)
    return pl.pallas_call(
        _mla_pre_body,
        grid=(s // tm,),
        in_specs=[
            pl.BlockSpec((tm, d), row),
            _const_spec((1, d)),
            _const_spec(w_in.shape),
            _const_spec((1, Q_LORA_RANK)),
            _const_spec(wq.shape),
            _const_spec((1, KV_LORA_RANK)),
            _const_spec(wk.shape),
            _const_spec(wv.shape),
            _const_spec((1, HEAD_PAD)),
            _const_spec((1, HEAD_PAD)),
            _const_spec(wp.shape),
            _const_spec((1, POOL_DIM)),
            pl.BlockSpec((tm, HEAD_PAD), row),
            pl.BlockSpec((tm, HEAD_PAD), row),
            pl.BlockSpec((tm, HEAD_PAD), row),
        ],
        out_specs=[
            pl.BlockSpec((tm, POOL_DIM), row),
            pl.BlockSpec((MLA_HEADS, HEAD_PAD, tm), lambda i: (0, 0, i)),
            pl.BlockSpec((MLA_HEADS, tm, HEAD_PAD), lambda i: (0, i, 0)),
            pl.BlockSpec((MLA_HEADS, V_ROWS, tm), lambda i: (0, 0, i)),
        ],
        out_shape=[
            jax.ShapeDtypeStruct((s, POOL_DIM), BF16),
            jax.ShapeDtypeStruct((MLA_HEADS, HEAD_PAD, s), BF16),
            jax.ShapeDtypeStruct((MLA_HEADS, s, HEAD_PAD), BF16),
            jax.ShapeDtypeStruct((MLA_HEADS, V_ROWS, s), BF16),
        ],
        scratch_shapes=[pltpu.VMEM((POOL_HALO, POOL_DIM), F32)],
        compiler_params=_params(("arbitrary",)),
        name="mla_pre",
    )(x, g.reshape(1, d), w_in, qa.reshape(1, -1), wq, kva.reshape(1, -1), wk, wv,
      qh, kh, wp, ps.reshape(1, -1), cos_t, slo_t, shi_t)


def _flash_body(qt_ref, k_ref, vt_ref, o_ref, s_refs, bm_refs, m_ref, acc_ref):
    i = pl.program_id(1)
    tq = qt_ref.shape[1]
    tk = ATTN_K_TILE
    m_ref[...] = jnp.full_like(m_ref, MASK_VALUE)
    acc_ref[...] = jnp.zeros_like(acc_ref)

    def scores(j, buf, c0=0):
        off = pl.multiple_of(j * tk, tk)
        st = jnp.dot(k_ref[pl.ds(off, tk), :], qt_ref[:, c0:], preferred_element_type=F32)
        s_refs[buf][:, c0:] = st
        bm_refs[buf][:, c0:] = jnp.max(st, axis=0, keepdims=True)

    def consume(j, buf, c0=0, diagonal=False):
        off = pl.multiple_of(j * tk, tk)
        st = s_refs[buf][:, c0:]
        if diagonal:
            row = lax.broadcasted_iota(jnp.int32, (tk, tk), 0)
            col = lax.broadcasted_iota(jnp.int32, (tk, tk), 1)
            square = jnp.where(row <= col, st[:, :tk], MASK_VALUE)
            st = square if tq - c0 == tk else jnp.concatenate([square, st[:, tk:]], axis=1)
            block_max = jnp.max(st, axis=0, keepdims=True)
        else:
            block_max = bm_refs[buf][:, c0:]
        m_old = m_ref[:, c0:]
        m_new = jnp.maximum(m_old, block_max)
        p = jnp.exp2(st - m_new).astype(BF16)
        alpha = jnp.exp2(m_old - m_new)
        vt = vt_ref[:, pl.ds(off, tk)]
        acc_ref[:, c0:] = acc_ref[:, c0:] * alpha + jnp.dot(vt, p, preferred_element_type=F32)
        m_ref[:, c0:] = m_new

    def phase(j, fill, drain):
        scores(j + 2, fill[0])
        consume(j, drain[0])
        scores(j + 3, fill[1])
        consume(j + 1, drain[1])

    scores(0, 0)
    scores(1, 1)

    def body(jj, carry):
        j = ATTN_BLOCKS_PER_Q * jj
        phase(j, (2, 3), (0, 1))
        phase(j + 2, (0, 1), (2, 3))
        return carry

    lax.fori_loop(0, i, body, 0)
    j = ATTN_BLOCKS_PER_Q * i
    scores(j + 2, 2, c0=2 * tk)
    consume(j, 0, c0=0, diagonal=True)
    scores(j + 3, 3, c0=3 * tk)
    consume(j + 1, 1, c0=tk, diagonal=True)
    consume(j + 2, 2, c0=2 * tk, diagonal=True)
    consume(j + 3, 3, c0=3 * tk, diagonal=True)

    acc = acc_ref[...]
    o_ref[...] = (acc[:V_HEAD_DIM] / acc[V_HEAD_DIM:V_HEAD_DIM + 1]).astype(BF16)


def _flash(qt, k, vt):
    h, s, dp = k.shape
    tq, tk = ATTN_Q_TILE, ATTN_K_TILE
    assert tq == ATTN_BLOCKS_PER_Q * tk
    n_buf = ATTN_BLOCKS_PER_Q
    return pl.pallas_call(
        _flash_body,
        grid=(h, s // tq),
        in_specs=[
            pl.BlockSpec((None, dp, tq), lambda hh, i: (hh, 0, i)),
            pl.BlockSpec((None, s, dp), lambda hh, i: (hh, 0, 0)),
            pl.BlockSpec((None, V_ROWS, s), lambda hh, i: (hh, 0, 0)),
        ],
        out_specs=pl.BlockSpec((V_HEAD_DIM, tq), lambda hh, i: (hh, i)),
        out_shape=jax.ShapeDtypeStruct((h * V_HEAD_DIM, s), BF16),
        scratch_shapes=[[pltpu.VMEM((tk, tq), F32)] * n_buf, [pltpu.VMEM((1, tq), F32)] * n_buf,
                        pltpu.VMEM((1, tq), F32), pltpu.VMEM((V_ROWS, tq), F32)],
        compiler_params=_params(("parallel", "arbitrary")),
        name="flash",
    )(qt, k, vt)


def _mix_out_body(x_ref, pool_ref, at_ref, wo_ref, o_ref):
    wo_pool = wo_ref[:POOL_DIM, :].astype(BF16)
    wo_attn = wo_ref[POOL_DIM:, :].astype(BF16)
    y = jnp.dot(pool_ref[...], wo_pool, preferred_element_type=F32)
    y = y + lax.dot_general(at_ref[...], wo_attn, (((0,), (0,)), ((), ())),
                            preferred_element_type=F32)
    o_ref[...] = x_ref[...] + y


def _mix_out(x, pool, attn_t, layer, wo):
    s, d = x.shape
    tm = TOKEN_TILE
    row = lambda i: (i, 0)
    return pl.pallas_call(
        _mix_out_body,
        grid=(s // tm,),
        in_specs=[
            pl.BlockSpec((tm, d), row),
            pl.BlockSpec((tm, POOL_DIM), row),
            pl.BlockSpec((attn_t.shape[0], tm), lambda i: (0, i)),
            _layer_spec(wo, layer),
        ],
        out_specs=pl.BlockSpec((tm, d), row),
        out_shape=jax.ShapeDtypeStruct((s, d), F32),
        compiler_params=_params(("parallel",)),
        name="mix_out",
    )(x, pool, attn_t, wo)


def _conv_body(x_ref, g_ref, w_in_ref, cw_ref, w_out_ref, o_ref, tail_ref):
    i = pl.program_id(0)
    tm, d = x_ref.shape

    @pl.when(i == 0)
    def _():
        tail_ref[...] = jnp.zeros_like(tail_ref)

    x = x_ref[...]
    hn = _rms(x, g_ref[...]).astype(w_in_ref.dtype)
    z = jnp.dot(hn, w_in_ref[...], preferred_element_type=F32)
    gb = z[:, :d]
    u = z[:, d:2 * d] * z[:, 2 * d:]
    ue = jnp.concatenate([tail_ref[...], u], axis=0)
    tail_ref[...] = u[tm - CONV_HALO:, :]
    cw = cw_ref[...]
    y = cw[CONV_WIDTH - 1:CONV_WIDTH] * u
    for back in range(1, CONV_WIDTH):
        tap = CONV_WIDTH - 1 - back
        y = y + cw[tap:tap + 1] * pltpu.roll(ue, back, axis=0)[CONV_HALO:, :]
    y = (gb * y).astype(w_out_ref.dtype)
    o_ref[...] = x + jnp.dot(y, w_out_ref[...], preferred_element_type=F32)


def _conv_mixer(x, layer, g, idx, w_in, cw, w_out):
    s, d = x.shape
    tm = TOKEN_TILE
    row = lambda i: (i, 0)
    g = _rows(g)
    return pl.pallas_call(
        _conv_body,
        grid=(s // tm,),
        in_specs=[
            pl.BlockSpec((tm, d), row),
            _layer_spec(g, layer),
            _layer_spec(w_in, idx),
            _layer_spec(cw, idx),
            _layer_spec(w_out, idx),
        ],
        out_specs=pl.BlockSpec((tm, d), row),
        out_shape=jax.ShapeDtypeStruct((s, d), F32),
        scratch_shapes=[pltpu.VMEM((CONV_HALO, d), F32)],
        compiler_params=_params(("arbitrary",)),
        name="conv_mixer",
    )(x, g, w_in, cw, w_out)


def _pad_heads(w, head_dim, take_from, take_n):
    rows = w.shape[0]
    w = w.reshape(rows, MLA_HEADS, head_dim)[:, :, take_from:take_from + take_n]
    w = jnp.pad(w, ((0, 0), (0, 0), (0, HEAD_PAD - take_n)))
    return w.reshape(rows, MLA_HEADS * HEAD_PAD)


def _rope_tables(s):
    pos = jnp.arange(s, dtype=F32)
    inv_freq = ROPE_THETA ** (-jnp.arange(0, QK_ROPE_DIM, 2, dtype=F32) / QK_ROPE_DIM)
    ang = pos[:, None] * inv_freq[None, :]
    cos = jnp.cos(ang)
    sin = jnp.sin(ang)
    half = QK_ROPE_DIM // 2
    ones = jnp.ones((s, QK_NOPE_DIM), F32)
    zeros_nope = jnp.zeros((s, QK_NOPE_DIM), F32)
    zeros_half = jnp.zeros((s, half), F32)
    zeros_tail = jnp.zeros((s, HEAD_PAD - QK_HEAD_DIM), F32)
    cos_t = jnp.concatenate([ones, cos, cos, zeros_tail], axis=-1)
    sin_lo_t = jnp.concatenate([zeros_nope, -sin, zeros_half, zeros_tail], axis=-1)
    sin_hi_t = jnp.concatenate([zeros_nope, zeros_half, sin, zeros_tail], axis=-1)
    return cos_t, sin_lo_t, sin_hi_t


def kernel(x, ffn1_norm, ffn1_w_gate, ffn1_w_up, ffn1_w_down, mix_norm, ffn2_norm, ffn2_w_gate, ffn2_w_up, ffn2_w_down, a_w_in, a_q_a_norm, a_w_q_up, a_kv_a_norm, a_w_kv_up, a_q_head_norm, a_k_head_norm, a_w_pool, a_pool_scale, a_w_out, c_w_in, c_conv_w, c_w_out):
    b, s, d = x.shape
    assert b == 1 and d == D_MODEL and s % TOKEN_TILE == 0 and s % ATTN_Q_TILE == 0
    h = x.reshape(s, d)
    cos_t, sin_lo_t, sin_hi_t = _rope_tables(s)
    head_pad = ((0, 0), (0, HEAD_PAD - QK_HEAD_DIM))

    for layer in range(DEPTH):
        h = _ffn(h, layer, ffn1_norm, ffn1_w_gate, ffn1_w_up, ffn1_w_down)
        i = layer // 2
        if layer % 2 == 0:
            w_in = a_w_in[i]
            c3 = POOL_DIM + Q_LORA_RANK + KV_LORA_RANK
            w_in_p = jnp.concatenate(
                [w_in[:, :c3], jnp.zeros((d, QK_NOPE_DIM), F32), w_in[:, c3:],
                 jnp.zeros((d, HEAD_PAD - QK_HEAD_DIM), F32)], axis=-1).astype(BF16)
            wq = _pad_heads(a_w_q_up[i], QK_HEAD_DIM, 0, QK_HEAD_DIM).astype(BF16)
            wk = _pad_heads(a_w_kv_up[i], QK_NOPE_DIM + V_HEAD_DIM, 0, QK_NOPE_DIM).astype(BF16)
            wv = _pad_heads(a_w_kv_up[i], QK_NOPE_DIM + V_HEAD_DIM, QK_NOPE_DIM, V_HEAD_DIM).astype(BF16)
            qh = jnp.pad(a_q_head_norm[i].reshape(1, -1), head_pad)
            kh = jnp.pad(a_k_head_norm[i].reshape(1, -1), head_pad)
            pool, qt, k, vt = _mla_pre(h, mix_norm[layer], w_in_p, a_q_a_norm[i], wq, a_kv_a_norm[i],
                                      wk, wv, qh, kh, a_w_pool[i].astype(BF16), a_pool_scale[i],
                                      cos_t, sin_lo_t, sin_hi_t)
            attn_t = _flash(qt, k, vt)
            h = _mix_out(h, pool, attn_t, i, a_w_out)
        else:
            h = _conv_mixer(h, layer, mix_norm, i, c_w_in, c_conv_w, c_w_out)
        h = _ffn(h, layer, ffn2_norm, ffn2_w_gate, ffn2_w_up, ffn2_w_down)
    return h.reshape(b, s, d)
```

```python
import functools

import jax
import jax.numpy as jnp
from jax import lax
from jax.experimental import pallas as pl
from jax.experimental.pallas import tpu as pltpu

D_MODEL = 1024
DEPTH = 4
D_FF = 2816
POOL_WINDOWS = (2, 4, 8, 16)
POOL_DIM = 512
POOL_GROUP_DIM = 128
MLA_HEADS = 8
QK_NOPE_DIM = 64
QK_ROPE_DIM = 32
QK_HEAD_DIM = 96
V_HEAD_DIM = 64
Q_LORA_RANK = 384
KV_LORA_RANK = 256
ROPE_THETA = 10000.0
CONV_WIDTH = 3
NORM_EPS = 1e-6

LANES = 128
HEAD_PAD = LANES
MIX_IN_PAD = POOL_DIM + Q_LORA_RANK + KV_LORA_RANK + HEAD_PAD
POOL_HALO = 16
CONV_HALO = 8
MASK_VALUE = -1e30
VMEM_LIMIT = 56 * 1024 * 1024

V_ROWS = 80
LOG2_E = 1.4426950408889634

TOKEN_TILE = 512
FFN_TILE = 512
FFN_CHUNKS = ((0, 1024), (1024, 2048), (2048, D_FF))
ATTN_Q_TILE = 1024
ATTN_K_TILE = 256
ATTN_BLOCKS_PER_Q = ATTN_Q_TILE // ATTN_K_TILE

F32 = jnp.float32
BF16 = jnp.bfloat16


def _rms(x, g, n=None):
    n = x.shape[-1] if n is None else n
    ms = jnp.sum(x * x, axis=-1, keepdims=True) * (1.0 / n)
    return x * lax.rsqrt(ms + NORM_EPS) * g


def _const_spec(shape):
    nd = len(shape)
    return pl.BlockSpec(shape, lambda *_: (0,) * nd, pipeline_mode=pl.Buffered(1))


def _layer_spec(stacked, layer):
    nd = stacked.ndim
    return pl.BlockSpec((None,) + stacked.shape[1:], lambda *_: (layer,) + (0,) * (nd - 1),
                        pipeline_mode=pl.Buffered(1))


def _rows(v):
    return v.reshape(v.shape[0], 1, v.shape[1])


def _params(sem):
    return pltpu.CompilerParams(dimension_semantics=sem, vmem_limit_bytes=VMEM_LIMIT)


def _ffn_body(x_ref, g_ref, wg_ref, wu_ref, wd_ref, o_ref):
    x = x_ref[...]
    xn = _rms(x, g_ref[...]).astype(wg_ref.dtype)
    y = None
    for lo, hi in FFN_CHUNKS:
        gate = jnp.dot(xn, wg_ref[:, lo:hi], preferred_element_type=F32)
        up = jnp.dot(xn, wu_ref[:, lo:hi], preferred_element_type=F32)
        h = (gate * jax.nn.sigmoid(gate) * up).astype(wd_ref.dtype)
        part = jnp.dot(h, wd_ref[lo:hi, :], preferred_element_type=F32)
        y = part if y is None else y + part
    o_ref[...] = x + 0.5 * y


def _ffn(x, layer, g, wg, wu, wd):
    s, d = x.shape
    tm = FFN_TILE
    g = _rows(g)
    return pl.pallas_call(
        _ffn_body,
        grid=(s // tm,),
        in_specs=[
            pl.BlockSpec((tm, d), lambda i: (i, 0)),
            _layer_spec(g, layer),
            _layer_spec(wg, layer),
            _layer_spec(wu, layer),
            _layer_spec(wd, layer),
        ],
        out_specs=pl.BlockSpec((tm, d), lambda i: (i, 0)),
        out_shape=jax.ShapeDtypeStruct((s, d), F32),
        compiler_params=_params(("parallel",)),
        name="ffn",
    )(x, g, wg, wu, wd)


def _gain_rope_t(xt, gain_col, cos_t, sin_t):
    half = QK_ROPE_DIM // 2
    a, b, c = QK_NOPE_DIM, QK_NOPE_DIM + half, QK_HEAD_DIM
    y = xt * gain_col
    x1, x2 = y[a:b], y[b:c]
    return jnp.concatenate([y[:a], x1 * cos_t - x2 * sin_t, x2 * cos_t + x1 * sin_t, y[c:]], axis=0)


def _mla_pre_body(x_ref, g_ref, w_in_ref, qa_ref, wq_ref, kva_ref, wk_ref, wv_ref,
                  qh_col_ref, kh_ref, kh_col_ref, wp_ref, ps_ref, cos_ref, sin_ref,
                  pool_ref, qt_ref, k_ref, vt_ref, tail_ref):
    i = pl.program_id(0)
    tm = x_ref.shape[0]

    @pl.when(i == 0)
    def _():
        tail_ref[...] = jnp.zeros_like(tail_ref)

    hn = _rms(x_ref[...], g_ref[...]).astype(BF16)
    z = jnp.dot(hn, w_in_ref[...], preferred_element_type=F32)
    c1 = POOL_DIM
    c2 = c1 + Q_LORA_RANK
    c3 = c2 + KV_LORA_RANK
    u = z[:, :c1]
    q_lat = z[:, c1:c2]
    kv_lat = z[:, c2:c3]
    k_rope = z[:, c3:]

    ue = jnp.concatenate([tail_ref[...], u], axis=0)
    tail_ref[...] = u[tm - POOL_HALO:, :]
    pos = i * tm + lax.broadcasted_iota(jnp.int32, (tm, 1), 0)
    pooled_out = []
    for g, w in enumerate(POOL_WINDOWS):
        a = ue[:, g * POOL_GROUP_DIM:(g + 1) * POOL_GROUP_DIM]
        shift = 1
        while shift < w:
            a = a + pltpu.roll(a, shift, axis=0)
            shift *= 2
        wsum = a[POOL_HALO:, :]
        cnt = jnp.minimum(pos + 1, w).astype(F32)
        pooled = wsum / cnt - u[:, g * POOL_GROUP_DIM:(g + 1) * POOL_GROUP_DIM]
        pooled_out.append(jnp.dot(pooled.astype(BF16), wp_ref[g], preferred_element_type=F32))
    pool_ref[...] = (jnp.concatenate(pooled_out, axis=-1) * ps_ref[...]).astype(BF16)

    cos_t = cos_ref[...]
    sin_t = sin_ref[...]

    qn = _rms(q_lat, qa_ref[...]).astype(BF16)
    q = jnp.dot(qn, wq_ref[...], preferred_element_type=F32)
    scale = QK_HEAD_DIM ** -0.5 * LOG2_E
    for h in range(MLA_HEADS):
        qt = q[:, h * HEAD_PAD:(h + 1) * HEAD_PAD].T
        ms = jnp.sum(qt * qt, axis=0, keepdims=True) * (1.0 / QK_HEAD_DIM)
        r = lax.rsqrt(ms + NORM_EPS) * scale
        qt_ref[h] = (_gain_rope_t(qt, qh_col_ref[...], cos_t, sin_t) * r).astype(BF16)

    kvn = _rms(kv_lat, kva_ref[...]).astype(BF16)
    kk = jnp.dot(kvn, wk_ref[...], preferred_element_type=F32)
    vv = jnp.dot(kvn, wv_ref[...], preferred_element_type=F32)
    ones_lane = (lax.broadcasted_iota(jnp.int32, (1, HEAD_PAD), 1) == V_HEAD_DIM).astype(F32)
    gk = kh_ref[...]
    rope_rot = _gain_rope_t(k_rope.T, kh_col_ref[...], cos_t, sin_t).T
    rope_ssq = jnp.sum(k_rope * k_rope, axis=-1, keepdims=True)
    for h in range(MLA_HEADS):
        kn = kk[:, h * HEAD_PAD:(h + 1) * HEAD_PAD]
        ms = (jnp.sum(kn * kn, axis=-1, keepdims=True) + rope_ssq) * (1.0 / QK_HEAD_DIM)
        k_ref[h] = ((kn * gk + rope_rot) * lax.rsqrt(ms + NORM_EPS)).astype(BF16)
        vh = vv[:, h * HEAD_PAD:(h + 1) * HEAD_PAD] + ones_lane
        vt_ref[h] = vh.T[:V_ROWS].astype(BF16)


def _mla_pre(x, g, w_in, qa, wq, kva, wk, wv, qh, kh, wp, ps, cos_t, sin_t):
    s, d = x.shape
    tm = TOKEN_TILE
    row = lambda i: (i, 0)
    half = QK_ROPE_DIM // 2
    return pl.pallas_call(
        _mla_pre_body,
        grid=(s // tm,),
        in_specs=[
            pl.BlockSpec((tm, d), row),
            _const_spec((1, d)),
            _const_spec(w_in.shape),
            _const_spec((1, Q_LORA_RANK)),
            _const_spec(wq.shape),
            _const_spec((1, KV_LORA_RANK)),
            _const_spec(wk.shape),
            _const_spec(wv.shape),
            _const_spec((HEAD_PAD, 1)),
            _const_spec((1, HEAD_PAD)),
            _const_spec((HEAD_PAD, 1)),
            _const_spec(wp.shape),
            _const_spec((1, POOL_DIM)),
            pl.BlockSpec((half, tm), lambda i: (0, i)),
            pl.BlockSpec((half, tm), lambda i: (0, i)),
        ],
        out_specs=[
            pl.BlockSpec((tm, POOL_DIM), row),
            pl.BlockSpec((MLA_HEADS, HEAD_PAD, tm), lambda i: (0, 0, i)),
            pl.BlockSpec((MLA_HEADS, tm, HEAD_PAD), lambda i: (0, i, 0)),
            pl.BlockSpec((MLA_HEADS, V_ROWS, tm), lambda i: (0, 0, i)),
        ],
        out_shape=[
            jax.ShapeDtypeStruct((s, POOL_DIM), BF16),
            jax.ShapeDtypeStruct((MLA_HEADS, HEAD_PAD, s), BF16),
            jax.ShapeDtypeStruct((MLA_HEADS, s, HEAD_PAD), BF16),
            jax.ShapeDtypeStruct((MLA_HEADS, V_ROWS, s), BF16),
        ],
        scratch_shapes=[pltpu.VMEM((POOL_HALO, POOL_DIM), F32)],
        compiler_params=_params(("arbitrary",)),
        name="mla_pre",
    )(x, g.reshape(1, d), w_in, qa.reshape(1, -1), wq, kva.reshape(1, -1), wk, wv,
      qh.reshape(HEAD_PAD, 1), kh, kh.reshape(HEAD_PAD, 1), wp, ps.reshape(1, -1), cos_t, sin_t)


def _flash_body(qt_ref, k_ref, vt_ref, o_ref, s_refs, bm_refs, m_ref, acc_ref):
    i = pl.program_id(1)
    tq = qt_ref.shape[1]
    tk = ATTN_K_TILE
    m_ref[...] = jnp.full_like(m_ref, MASK_VALUE)
    acc_ref[...] = jnp.zeros_like(acc_ref)

    def scores(j, buf, c0=0):
        off = pl.multiple_of(j * tk, tk)
        st = jnp.dot(k_ref[pl.ds(off, tk), :], qt_ref[:, c0:], preferred_element_type=F32)
        s_refs[buf][:, c0:] = st
        bm_refs[buf][:, c0:] = jnp.max(st, axis=0, keepdims=True)

    def consume(j, buf, c0=0, diagonal=False):
        off = pl.multiple_of(j * tk, tk)
        st = s_refs[buf][:, c0:]
        if diagonal:
            row = lax.broadcasted_iota(jnp.int32, (tk, tk), 0)
            col = lax.broadcasted_iota(jnp.int32, (tk, tk), 1)
            square = jnp.where(row <= col, st[:, :tk], MASK_VALUE)
            st = square if tq - c0 == tk else jnp.concatenate([square, st[:, tk:]], axis=1)
            block_max = jnp.max(st, axis=0, keepdims=True)
        else:
            block_max = bm_refs[buf][:, c0:]
        m_old = m_ref[:, c0:]
        m_new = jnp.maximum(m_old, block_max)
        p = jnp.exp2(st - m_new).astype(BF16)
        alpha = jnp.exp2(m_old - m_new)
        vt = vt_ref[:, pl.ds(off, tk)]
        acc_ref[:, c0:] = acc_ref[:, c0:] * alpha + jnp.dot(vt, p, preferred_element_type=F32)
        m_ref[:, c0:] = m_new

    nb = ATTN_BLOCKS_PER_Q
    pairs = ((0, 1), (2, 3))

    scores(0, 0)
    scores(1, 1)

    def body(jj, carry):
        j = nb * jj
        for ph in range(nb // 2):
            drain, fill = pairs[ph % 2], pairs[(ph + 1) % 2]
            b = j + 2 * ph
            scores(b + 2, fill[0])
            consume(b, drain[0])
            scores(b + 3, fill[1])
            consume(b + 1, drain[1])
        return carry

    lax.fori_loop(0, i, body, 0)
    j = nb * i
    for ph in range(nb // 2):
        drain, fill = pairs[ph % 2], pairs[(ph + 1) % 2]
        d = 2 * ph
        last = d + 2 >= nb
        if not last:
            scores(j + d + 2, fill[0], c0=(d + 2) * tk)
        consume(j + d, drain[0], c0=d * tk, diagonal=True)
        if not last:
            scores(j + d + 3, fill[1], c0=(d + 3) * tk)
        consume(j + d + 1, drain[1], c0=(d + 1) * tk, diagonal=True)

    acc = acc_ref[...]
    o_ref[...] = (acc[:V_HEAD_DIM] / acc[V_HEAD_DIM:V_HEAD_DIM + 1]).astype(BF16)


def _flash(qt, k, vt):
    h, s, dp = k.shape
    tq, tk = ATTN_Q_TILE, ATTN_K_TILE
    assert tq == ATTN_BLOCKS_PER_Q * tk and ATTN_BLOCKS_PER_Q % 4 == 0
    n_buf = 4
    return pl.pallas_call(
        _flash_body,
        grid=(h, s // tq),
        in_specs=[
            pl.BlockSpec((None, dp, tq), lambda hh, i: (hh, 0, i)),
            pl.BlockSpec((None, s, dp), lambda hh, i: (hh, 0, 0)),
            pl.BlockSpec((None, V_ROWS, s), lambda hh, i: (hh, 0, 0)),
        ],
        out_specs=pl.BlockSpec((V_HEAD_DIM, tq), lambda hh, i: (hh, i)),
        out_shape=jax.ShapeDtypeStruct((h * V_HEAD_DIM, s), BF16),
        scratch_shapes=[[pltpu.VMEM((tk, tq), F32)] * n_buf, [pltpu.VMEM((1, tq), F32)] * n_buf,
                        pltpu.VMEM((1, tq), F32), pltpu.VMEM((V_ROWS, tq), F32)],
        compiler_params=_params(("parallel", "arbitrary")),
        name="flash",
    )(qt, k, vt)


def _mix_out_body(x_ref, pool_ref, at_ref, wo_ref, o_ref):
    wo_pool = wo_ref[:POOL_DIM, :].astype(BF16)
    wo_attn = wo_ref[POOL_DIM:, :].astype(BF16)
    y = jnp.dot(pool_ref[...], wo_pool, preferred_element_type=F32)
    y = y + lax.dot_general(at_ref[...], wo_attn, (((0,), (0,)), ((), ())),
                            preferred_element_type=F32)
    o_ref[...] = x_ref[...] + y


def _mix_out(x, pool, attn_t, layer, wo):
    s, d = x.shape
    tm = TOKEN_TILE
    row = lambda i: (i, 0)
    return pl.pallas_call(
        _mix_out_body,
        grid=(s // tm,),
        in_specs=[
            pl.BlockSpec((tm, d), row),
            pl.BlockSpec((tm, POOL_DIM), row),
            pl.BlockSpec((attn_t.shape[0], tm), lambda i: (0, i)),
            _layer_spec(wo, layer),
        ],
        out_specs=pl.BlockSpec((tm, d), row),
        out_shape=jax.ShapeDtypeStruct((s, d), F32),
        compiler_params=_params(("parallel",)),
        name="mix_out",
    )(x, pool, attn_t, wo)


def _conv_body(x_ref, g_ref, w_in_ref, cw_ref, w_out_ref, o_ref, tail_ref):
    i = pl.program_id(0)
    tm, d = x_ref.shape

    @pl.when(i == 0)
    def _():
        tail_ref[...] = jnp.zeros_like(tail_ref)

    x = x_ref[...]
    hn = _rms(x, g_ref[...]).astype(w_in_ref.dtype)
    z = jnp.dot(hn, w_in_ref[...], preferred_element_type=F32)
    gb = z[:, :d]
    u = z[:, d:2 * d] * z[:, 2 * d:]
    ue = jnp.concatenate([tail_ref[...], u], axis=0)
    tail_ref[...] = u[tm - CONV_HALO:, :]
    cw = cw_ref[...]
    y = cw[CONV_WIDTH - 1:CONV_WIDTH] * u
    for back in range(1, CONV_WIDTH):
        tap = CONV_WIDTH - 1 - back
        y = y + cw[tap:tap + 1] * pltpu.roll(ue, back, axis=0)[CONV_HALO:, :]
    y = (gb * y).astype(w_out_ref.dtype)
    o_ref[...] = x + jnp.dot(y, w_out_ref[...], preferred_element_type=F32)


def _conv_mixer(x, layer, g, idx, w_in, cw, w_out):
    s, d = x.shape
    tm = TOKEN_TILE
    row = lambda i: (i, 0)
    g = _rows(g)
    return pl.pallas_call(
        _conv_body,
        grid=(s // tm,),
        in_specs=[
            pl.BlockSpec((tm, d), row),
            _layer_spec(g, layer),
            _layer_spec(w_in, idx),
            _layer_spec(cw, idx),
            _layer_spec(w_out, idx),
        ],
        out_specs=pl.BlockSpec((tm, d), row),
        out_shape=jax.ShapeDtypeStruct((s, d), F32),
        scratch_shapes=[pltpu.VMEM((CONV_HALO, d), F32)],
        compiler_params=_params(("arbitrary",)),
        name="conv_mixer",
    )(x, g, w_in, cw, w_out)


def _pad_heads(w, head_dim, take_from, take_n):
    rows = w.shape[0]
    w = w.reshape(rows, MLA_HEADS, head_dim)[:, :, take_from:take_from + take_n]
    w = jnp.pad(w, ((0, 0), (0, 0), (0, HEAD_PAD - take_n)))
    return w.reshape(rows, MLA_HEADS * HEAD_PAD)


def _rope_tables(s):
    pos = jnp.arange(s, dtype=F32)
    inv_freq = ROPE_THETA ** (-jnp.arange(0, QK_ROPE_DIM, 2, dtype=F32) / QK_ROPE_DIM)
    ang = inv_freq[:, None] * pos[None, :]
    return jnp.cos(ang), jnp.sin(ang)


def kernel(x, ffn1_norm, ffn1_w_gate, ffn1_w_up, ffn1_w_down, mix_norm, ffn2_norm, ffn2_w_gate, ffn2_w_up, ffn2_w_down, a_w_in, a_q_a_norm, a_w_q_up, a_kv_a_norm, a_w_kv_up, a_q_head_norm, a_k_head_norm, a_w_pool, a_pool_scale, a_w_out, c_w_in, c_conv_w, c_w_out):
    b, s, d = x.shape
    assert b == 1 and d == D_MODEL and s % TOKEN_TILE == 0 and s % ATTN_Q_TILE == 0
    h = x.reshape(s, d)
    cos_t, sin_t = _rope_tables(s)
    head_pad = ((0, 0), (0, HEAD_PAD - QK_HEAD_DIM))

    for layer in range(DEPTH):
        h = _ffn(h, layer, ffn1_norm, ffn1_w_gate, ffn1_w_up, ffn1_w_down)
        i = layer // 2
        if layer % 2 == 0:
            w_in = a_w_in[i]
            c3 = POOL_DIM + Q_LORA_RANK + KV_LORA_RANK
            w_in_p = jnp.concatenate(
                [w_in[:, :c3], jnp.zeros((d, QK_NOPE_DIM), F32), w_in[:, c3:],
                 jnp.zeros((d, HEAD_PAD - QK_HEAD_DIM), F32)], axis=-1).astype(BF16)
            wq = _pad_heads(a_w_q_up[i], QK_HEAD_DIM, 0, QK_HEAD_DIM).astype(BF16)
            wk = _pad_heads(a_w_kv_up[i], QK_NOPE_DIM + V_HEAD_DIM, 0, QK_NOPE_DIM).astype(BF16)
            wv = _pad_heads(a_w_kv_up[i], QK_NOPE_DIM + V_HEAD_DIM, QK_NOPE_DIM, V_HEAD_DIM).astype(BF16)
            qh = jnp.pad(a_q_head_norm[i].reshape(1, -1), head_pad)
            kh = jnp.pad(a_k_head_norm[i].reshape(1, -1), head_pad)
            pool, qt, k, vt = _mla_pre(h, mix_norm[layer], w_in_p, a_q_a_norm[i], wq, a_kv_a_norm[i],
                                      wk, wv, qh, kh, a_w_pool[i].astype(BF16), a_pool_scale[i],
                                      cos_t, sin_t)
            attn_t = _flash(qt, k, vt)
            h = _mix_out(h, pool, attn_t, i, a_w_out)
        else:
            h = _conv_mixer(h, layer, mix_norm, i, c_w_in, c_conv_w, c_w_out)
        h = _ffn(h, layer, ffn2_norm, ffn2_w_gate, ffn2_w_up, ffn2_w_down)
    return h.reshape(b, s, d)
```

```python
import functools

import jax
import jax.numpy as jnp
from jax import lax
from jax.experimental import pallas as pl
from jax.experimental.pallas import tpu as pltpu

D_MODEL = 1024
DEPTH = 4
D_FF = 2816
POOL_WINDOWS = (2, 4, 8, 16)
POOL_DIM = 512
POOL_GROUP_DIM = 128
MLA_HEADS = 8
QK_NOPE_DIM = 64
QK_ROPE_DIM = 32
QK_HEAD_DIM = 96
V_HEAD_DIM = 64
Q_LORA_RANK = 384
KV_LORA_RANK = 256
ROPE_THETA = 10000.0
CONV_WIDTH = 3
NORM_EPS = 1e-6

LANES = 128
HEAD_PAD = LANES
MIX_IN_PAD = POOL_DIM + Q_LORA_RANK + KV_LORA_RANK + HEAD_PAD
POOL_HALO = 16
CONV_HALO = 8
MASK_VALUE = -1e30
VMEM_LIMIT = 56 * 1024 * 1024

V_ROWS = 80
LOG2_E = 1.4426950408889634

TOKEN_TILE = 512
FFN_TILE = 512
FFN_CHUNKS = ((0, 1024), (1024, 2048), (2048, D_FF))
ATTN_Q_TILE = 1024
ATTN_K_TILE = 256
ATTN_BLOCKS_PER_Q = ATTN_Q_TILE // ATTN_K_TILE
ATTN_BLOCKS_PER_TRIP = 4 * ATTN_BLOCKS_PER_Q

F32 = jnp.float32
BF16 = jnp.bfloat16


def _rms(x, g, n=None):
    n = x.shape[-1] if n is None else n
    ms = jnp.sum(x * x, axis=-1, keepdims=True) * (1.0 / n)
    return x * lax.rsqrt(ms + NORM_EPS) * g


def _const_spec(shape):
    nd = len(shape)
    return pl.BlockSpec(shape, lambda *_: (0,) * nd, pipeline_mode=pl.Buffered(1))


def _layer_spec(stacked, layer):
    nd = stacked.ndim
    return pl.BlockSpec((None,) + stacked.shape[1:], lambda *_: (layer,) + (0,) * (nd - 1),
                        pipeline_mode=pl.Buffered(1))


def _rows(v):
    return v.reshape(v.shape[0], 1, v.shape[1])


def _params(sem):
    return pltpu.CompilerParams(dimension_semantics=sem, vmem_limit_bytes=VMEM_LIMIT)


def _ffn_half_step(x, g_ref, wg_ref, wu_ref, wd_ref):
    xn = _rms(x, g_ref[...]).astype(wg_ref.dtype)
    y = None
    for lo, hi in FFN_CHUNKS:
        gate = jnp.dot(xn, wg_ref[:, lo:hi], preferred_element_type=F32)
        up = jnp.dot(xn, wu_ref[:, lo:hi], preferred_element_type=F32)
        h = (gate * jax.nn.sigmoid(gate) * up).astype(wd_ref.dtype)
        part = jnp.dot(h, wd_ref[lo:hi, :], preferred_element_type=F32)
        y = part if y is None else y + part
    return x + 0.5 * y


def _ffn_body(x_ref, g_ref, wg_ref, wu_ref, wd_ref, o_ref):
    o_ref[...] = _ffn_half_step(x_ref[...], g_ref, wg_ref, wu_ref, wd_ref)


def _mix_ffn_body(x_ref, pool_ref, at_ref, wo_ref, g_ref, wg_ref, wu_ref, wd_ref, o_ref):
    wo_pool = wo_ref[:POOL_DIM, :].astype(BF16)
    wo_attn = wo_ref[POOL_DIM:, :].astype(BF16)
    y = jnp.dot(pool_ref[...], wo_pool, preferred_element_type=F32)
    y = y + lax.dot_general(at_ref[...], wo_attn, (((0,), (0,)), ((), ())),
                            preferred_element_type=F32)
    o_ref[...] = _ffn_half_step(x_ref[...] + y, g_ref, wg_ref, wu_ref, wd_ref)


def _ffn(x, layer, g, wg, wu, wd, mix=None):
    s, d = x.shape
    tm = FFN_TILE
    g = _rows(g)
    row = lambda i: (i, 0)
    ffn_specs = [_layer_spec(g, layer), _layer_spec(wg, layer), _layer_spec(wu, layer),
                 _layer_spec(wd, layer)]
    if mix is None:
        body, name = _ffn_body, "ffn"
        operands = (x, g, wg, wu, wd)
        in_specs = [pl.BlockSpec((tm, d), row)] + ffn_specs
    else:
        pool, attn_t, idx, wo = mix
        body, name = _mix_ffn_body, "mix_ffn"
        operands = (x, pool, attn_t, wo, g, wg, wu, wd)
        in_specs = [pl.BlockSpec((tm, d), row), pl.BlockSpec((tm, POOL_DIM), row),
                    pl.BlockSpec((attn_t.shape[0], tm), lambda i: (0, i)),
                    _layer_spec(wo, idx)] + ffn_specs
    return pl.pallas_call(
        body,
        grid=(s // tm,),
        in_specs=in_specs,
        out_specs=pl.BlockSpec((tm, d), row),
        out_shape=jax.ShapeDtypeStruct((s, d), F32),
        compiler_params=_params(("parallel",)),
        name=name,
    )(*operands)


def _gain_rope_t(xt, gain_col, cos_t, sin_t):
    half = QK_ROPE_DIM // 2
    a, b, c = QK_NOPE_DIM, QK_NOPE_DIM + half, QK_HEAD_DIM
    y = xt * gain_col
    x1, x2 = y[a:b], y[b:c]
    return jnp.concatenate([y[:a], x1 * cos_t - x2 * sin_t, x2 * cos_t + x1 * sin_t, y[c:]], axis=0)


def _mla_pre_body(x_ref, g_ref, w_in_ref, qa_ref, wq_ref, kva_ref, wk_ref, wv_ref,
                  qh_col_ref, kh_ref, kh_col_ref, wp_ref, ps_ref, cos_ref, sin_ref,
                  pool_ref, qt_ref, k_ref, vt_ref, tail_ref):
    i = pl.program_id(0)
    tm = x_ref.shape[0]

    @pl.when(i == 0)
    def _():
        tail_ref[...] = jnp.zeros_like(tail_ref)

    hn = _rms(x_ref[...], g_ref[...]).astype(BF16)
    z = jnp.dot(hn, w_in_ref[...], preferred_element_type=F32)
    c1 = POOL_DIM
    c2 = c1 + Q_LORA_RANK
    c3 = c2 + KV_LORA_RANK
    u = z[:, :c1]
    q_lat = z[:, c1:c2]
    kv_lat = z[:, c2:c3]
    k_rope = z[:, c3:]

    ue = jnp.concatenate([tail_ref[...], u], axis=0)
    tail_ref[...] = u[tm - POOL_HALO:, :]
    pos = i * tm + lax.broadcasted_iota(jnp.int32, (tm, 1), 0)
    pooled_out = []
    for g, w in enumerate(POOL_WINDOWS):
        a = ue[:, g * POOL_GROUP_DIM:(g + 1) * POOL_GROUP_DIM]
        shift = 1
        while shift < w:
            a = a + pltpu.roll(a, shift, axis=0)
            shift *= 2
        wsum = a[POOL_HALO:, :]
        cnt = jnp.minimum(pos + 1, w).astype(F32)
        pooled = wsum / cnt - u[:, g * POOL_GROUP_DIM:(g + 1) * POOL_GROUP_DIM]
        pooled_out.append(jnp.dot(pooled.astype(BF16), wp_ref[g], preferred_element_type=F32))
    pool_ref[...] = (jnp.concatenate(pooled_out, axis=-1) * ps_ref[...]).astype(BF16)

    cos_t = cos_ref[...]
    sin_t = sin_ref[...]

    qn = _rms(q_lat, qa_ref[...]).astype(BF16)
    q = jnp.dot(qn, wq_ref[...], preferred_element_type=F32)
    scale = QK_HEAD_DIM ** -0.5 * LOG2_E
    for h in range(MLA_HEADS):
        qt = q[:, h * HEAD_PAD:(h + 1) * HEAD_PAD].T
        ms = jnp.sum(qt * qt, axis=0, keepdims=True) * (1.0 / QK_HEAD_DIM)
        r = lax.rsqrt(ms + NORM_EPS) * scale
        qt_ref[h] = (_gain_rope_t(qt, qh_col_ref[...], cos_t, sin_t) * r).astype(BF16)

    kvn = _rms(kv_lat, kva_ref[...]).astype(BF16)
    kk = jnp.dot(kvn, wk_ref[...], preferred_element_type=F32)
    vv = jnp.dot(kvn, wv_ref[...], preferred_element_type=F32)
    ones_lane = (lax.broadcasted_iota(jnp.int32, (1, HEAD_PAD), 1) == V_HEAD_DIM).astype(F32)
    gk = kh_ref[...]
    rope_rot = _gain_rope_t(k_rope.T, kh_col_ref[...], cos_t, sin_t).T
    rope_ssq = jnp.sum(k_rope * k_rope, axis=-1, keepdims=True)
    for h in range(MLA_HEADS):
        kn = kk[:, h * HEAD_PAD:(h + 1) * HEAD_PAD]
        ms = (jnp.sum(kn * kn, axis=-1, keepdims=True) + rope_ssq) * (1.0 / QK_HEAD_DIM)
        k_ref[h] = ((kn * gk + rope_rot) * lax.rsqrt(ms + NORM_EPS)).astype(BF16)
        vh = vv[:, h * HEAD_PAD:(h + 1) * HEAD_PAD] + ones_lane
        vt_ref[h] = vh.T[:V_ROWS].astype(BF16)


def _mla_pre(x, g, w_in, qa, wq, kva, wk, wv, qh, kh, wp, ps, cos_t, sin_t):
    s, d = x.shape
    tm = TOKEN_TILE
    row = lambda i: (i, 0)
    half = QK_ROPE_DIM // 2
    return pl.pallas_call(
        _mla_pre_body,
        grid=(s // tm,),
        in_specs=[
            pl.BlockSpec((tm, d), row),
            _const_spec((1, d)),
            _const_spec(w_in.shape),
            _const_spec((1, Q_LORA_RANK)),
            _const_spec(wq.shape),
            _const_spec((1, KV_LORA_RANK)),
            _const_spec(wk.shape),
            _const_spec(wv.shape),
            _const_spec((HEAD_PAD, 1)),
            _const_spec((1, HEAD_PAD)),
            _const_spec((HEAD_PAD, 1)),
            _const_spec(wp.shape),
            _const_spec((1, POOL_DIM)),
            pl.BlockSpec((half, tm), lambda i: (0, i)),
            pl.BlockSpec((half, tm), lambda i: (0, i)),
        ],
        out_specs=[
            pl.BlockSpec((tm, POOL_DIM), row),
            pl.BlockSpec((MLA_HEADS, HEAD_PAD, tm), lambda i: (0, 0, i)),
            pl.BlockSpec((MLA_HEADS, tm, HEAD_PAD), lambda i: (0, i, 0)),
            pl.BlockSpec((MLA_HEADS, V_ROWS, tm), lambda i: (0, 0, i)),
        ],
        out_shape=[
            jax.ShapeDtypeStruct((s, POOL_DIM), BF16),
            jax.ShapeDtypeStruct((MLA_HEADS, HEAD_PAD, s), BF16),
            jax.ShapeDtypeStruct((MLA_HEADS, s, HEAD_PAD), BF16),
            jax.ShapeDtypeStruct((MLA_HEADS, V_ROWS, s), BF16),
        ],
        scratch_shapes=[pltpu.VMEM((POOL_HALO, POOL_DIM), F32)],
        compiler_params=_params(("arbitrary",)),
        name="mla_pre",
    )(x, g.reshape(1, d), w_in, qa.reshape(1, -1), wq, kva.reshape(1, -1), wk, wv,
      qh.reshape(HEAD_PAD, 1), kh, kh.reshape(HEAD_PAD, 1), wp, ps.reshape(1, -1), cos_t, sin_t)


def _flash_body(qt_ref, k_ref, vt_ref, o_ref, s_refs, bm_refs, m_ref, acc_ref):
    i = pl.program_id(1)
    tq = qt_ref.shape[1]
    tk = ATTN_K_TILE
    m_ref[...] = jnp.full_like(m_ref, MASK_VALUE)
    acc_ref[...] = jnp.zeros_like(acc_ref)

    def scores(j, buf, c0=0):
        off = pl.multiple_of(j * tk, tk)
        st = jnp.dot(k_ref[pl.ds(off, tk), :], qt_ref[:, c0:], preferred_element_type=F32)
        s_refs[buf][:, c0:] = st
        bm_refs[buf][:, c0:] = jnp.max(st, axis=0, keepdims=True)

    def consume(j, buf, c0=0, diagonal=False):
        off = pl.multiple_of(j * tk, tk)
        st = s_refs[buf][:, c0:]
        if diagonal:
            row = lax.broadcasted_iota(jnp.int32, (tk, tk), 0)
            col = lax.broadcasted_iota(jnp.int32, (tk, tk), 1)
            square = jnp.where(row <= col, st[:, :tk], MASK_VALUE)
            st = square if tq - c0 == tk else jnp.concatenate([square, st[:, tk:]], axis=1)
            block_max = jnp.max(st, axis=0, keepdims=True)
        else:
            block_max = bm_refs[buf][:, c0:]
        m_old = m_ref[:, c0:]
        m_new = jnp.maximum(m_old, block_max)
        p = jnp.exp2(st - m_new).astype(BF16)
        alpha = jnp.exp2(m_old - m_new)
        vt = vt_ref[:, pl.ds(off, tk)]
        acc_ref[:, c0:] = acc_ref[:, c0:] * alpha + jnp.dot(vt, p, preferred_element_type=F32)
        m_ref[:, c0:] = m_new

    nb = ATTN_BLOCKS_PER_Q
    pairs = ((0, 1), (2, 3))

    scores(0, 0)
    scores(1, 1)

    def run_blocks(j, count):
        for ph in range(count // 2):
            drain, fill = pairs[ph % 2], pairs[(ph + 1) % 2]
            b = j + 2 * ph
            scores(b + 2, fill[0])
            consume(b, drain[0])
            scores(b + 3, fill[1])
            consume(b + 1, drain[1])

    full = nb * i
    trips = full // ATTN_BLOCKS_PER_TRIP

    def body(jj, carry):
        run_blocks(ATTN_BLOCKS_PER_TRIP * jj, ATTN_BLOCKS_PER_TRIP)
        return carry

    lax.fori_loop(0, trips, body, 0)
    for rest in range(nb, ATTN_BLOCKS_PER_TRIP, nb):
        @pl.when(full - trips * ATTN_BLOCKS_PER_TRIP == rest)
        def _():
            run_blocks(trips * ATTN_BLOCKS_PER_TRIP, rest)
    j = full
    for ph in range(nb // 2):
        drain, fill = pairs[ph % 2], pairs[(ph + 1) % 2]
        d = 2 * ph
        last = d + 2 >= nb
        if not last:
            scores(j + d + 2, fill[0], c0=(d + 2) * tk)
        consume(j + d, drain[0], c0=d * tk, diagonal=True)
        if not last:
            scores(j + d + 3, fill[1], c0=(d + 3) * tk)
        consume(j + d + 1, drain[1], c0=(d + 1) * tk, diagonal=True)

    acc = acc_ref[...]
    o_ref[...] = (acc[:V_HEAD_DIM] / acc[V_HEAD_DIM:V_HEAD_DIM + 1]).astype(BF16)


def _flash(qt, k, vt):
    h, s, dp = k.shape
    tq, tk = ATTN_Q_TILE, ATTN_K_TILE
    assert tq == ATTN_BLOCKS_PER_Q * tk and ATTN_BLOCKS_PER_Q % 4 == 0
    n_buf = 4
    return pl.pallas_call(
        _flash_body,
        grid=(h, s // tq),
        in_specs=[
            pl.BlockSpec((None, dp, tq), lambda hh, i: (hh, 0, i)),
            pl.BlockSpec((None, s, dp), lambda hh, i: (hh, 0, 0)),
            pl.BlockSpec((None, V_ROWS, s), lambda hh, i: (hh, 0, 0)),
        ],
        out_specs=pl.BlockSpec((V_HEAD_DIM, tq), lambda hh, i: (hh, i)),
        out_shape=jax.ShapeDtypeStruct((h * V_HEAD_DIM, s), BF16),
        scratch_shapes=[[pltpu.VMEM((tk, tq), F32)] * n_buf, [pltpu.VMEM((1, tq), F32)] * n_buf,
                        pltpu.VMEM((1, tq), F32), pltpu.VMEM((V_ROWS, tq), F32)],
        compiler_params=_params(("parallel", "arbitrary")),
        name="flash",
    )(qt, k, vt)


def _conv_body(x_ref, g_ref, w_in_ref, cw_ref, w_out_ref, o_ref, tail_ref):
    i = pl.program_id(0)
    tm, d = x_ref.shape

    @pl.when(i == 0)
    def _():
        tail_ref[...] = jnp.zeros_like(tail_ref)

    x = x_ref[...]
    hn = _rms(x, g_ref[...]).astype(w_in_ref.dtype)
    z = jnp.dot(hn, w_in_ref[...], preferred_element_type=F32)
    gb = z[:, :d]
    u = z[:, d:2 * d] * z[:, 2 * d:]
    ue = jnp.concatenate([tail_ref[...], u], axis=0)
    tail_ref[...] = u[tm - CONV_HALO:, :]
    cw = cw_ref[...]
    y = cw[CONV_WIDTH - 1:CONV_WIDTH] * u
    for back in range(1, CONV_WIDTH):
        tap = CONV_WIDTH - 1 - back
        y = y + cw[tap:tap + 1] * pltpu.roll(ue, back, axis=0)[CONV_HALO:, :]
    y = (gb * y).astype(w_out_ref.dtype)
    o_ref[...] = x + jnp.dot(y, w_out_ref[...], preferred_element_type=F32)


def _conv_mixer(x, layer, g, idx, w_in, cw, w_out):
    s, d = x.shape
    tm = TOKEN_TILE
    row = lambda i: (i, 0)
    g = _rows(g)
    return pl.pallas_call(
        _conv_body,
        grid=(s // tm,),
        in_specs=[
            pl.BlockSpec((tm, d), row),
            _layer_spec(g, layer),
            _layer_spec(w_in, idx),
            _layer_spec(cw, idx),
            _layer_spec(w_out, idx),
        ],
        out_specs=pl.BlockSpec((tm, d), row),
        out_shape=jax.ShapeDtypeStruct((s, d), F32),
        scratch_shapes=[pltpu.VMEM((CONV_HALO, d), F32)],
        compiler_params=_params(("arbitrary",)),
        name="conv_mixer",
    )(x, g, w_in, cw, w_out)


def _pad_heads(w, head_dim, take_from, take_n):
    rows = w.shape[0]
    w = w.reshape(rows, MLA_HEADS, head_dim)[:, :, take_from:take_from + take_n]
    w = jnp.pad(w, ((0, 0), (0, 0), (0, HEAD_PAD - take_n)))
    return w.reshape(rows, MLA_HEADS * HEAD_PAD)


def _rope_tables(s):
    pos = jnp.arange(s, dtype=F32)
    inv_freq = ROPE_THETA ** (-jnp.arange(0, QK_ROPE_DIM, 2, dtype=F32) / QK_ROPE_DIM)
    ang = inv_freq[:, None] * pos[None, :]
    return jnp.cos(ang), jnp.sin(ang)


def kernel(x, ffn1_norm, ffn1_w_gate, ffn1_w_up, ffn1_w_down, mix_norm, ffn2_norm, ffn2_w_gate, ffn2_w_up, ffn2_w_down, a_w_in, a_q_a_norm, a_w_q_up, a_kv_a_norm, a_w_kv_up, a_q_head_norm, a_k_head_norm, a_w_pool, a_pool_scale, a_w_out, c_w_in, c_conv_w, c_w_out):
    b, s, d = x.shape
    assert b == 1 and d == D_MODEL and s % TOKEN_TILE == 0 and s % ATTN_Q_TILE == 0
    h = x.reshape(s, d)
    cos_t, sin_t = _rope_tables(s)
    head_pad = ((0, 0), (0, HEAD_PAD - QK_HEAD_DIM))

    for layer in range(DEPTH):
        h = _ffn(h, layer, ffn1_norm, ffn1_w_gate, ffn1_w_up, ffn1_w_down)
        i = layer // 2
        if layer % 2 == 0:
            w_in = a_w_in[i]
            c3 = POOL_DIM + Q_LORA_RANK + KV_LORA_RANK
            w_in_p = jnp.concatenate(
                [w_in[:, :c3], jnp.zeros((d, QK_NOPE_DIM), F32), w_in[:, c3:],
                 jnp.zeros((d, HEAD_PAD - QK_HEAD_DIM), F32)], axis=-1).astype(BF16)
            wq = _pad_heads(a_w_q_up[i], QK_HEAD_DIM, 0, QK_HEAD_DIM).astype(BF16)
            wk = _pad_heads(a_w_kv_up[i], QK_NOPE_DIM + V_HEAD_DIM, 0, QK_NOPE_DIM).astype(BF16)
            wv = _pad_heads(a_w_kv_up[i], QK_NOPE_DIM + V_HEAD_DIM, QK_NOPE_DIM, V_HEAD_DIM).astype(BF16)
            qh = jnp.pad(a_q_head_norm[i].reshape(1, -1), head_pad)
            kh = jnp.pad(a_k_head_norm[i].reshape(1, -1), head_pad)
            pool, qt, k, vt = _mla_pre(h, mix_norm[layer], w_in_p, a_q_a_norm[i], wq, a_kv_a_norm[i],
                                      wk, wv, qh, kh, a_w_pool[i].astype(BF16), a_pool_scale[i],
                                      cos_t, sin_t)
            attn_t = _flash(qt, k, vt)
            mix = (pool, attn_t, i, a_w_out)
        else:
            h = _conv_mixer(h, layer, mix_norm, i, c_w_in, c_conv_w, c_w_out)
            mix = None
        h = _ffn(h, layer, ffn2_norm, ffn2_w_gate, ffn2_w_up, ffn2_w_down, mix=mix)
    return h.reshape(b, s, d)
```

```python
import functools

import jax
import jax.numpy as jnp
from jax import lax
from jax.experimental import pallas as pl
from jax.experimental.pallas import tpu as pltpu

D_MODEL = 1024
DEPTH = 4
D_FF = 2816
POOL_WINDOWS = (2, 4, 8, 16)
POOL_DIM = 512
POOL_GROUP_DIM = 128
MLA_HEADS = 8
QK_NOPE_DIM = 64
QK_ROPE_DIM = 32
QK_HEAD_DIM = 96
V_HEAD_DIM = 64
Q_LORA_RANK = 384
KV_LORA_RANK = 256
ROPE_THETA = 10000.0
CONV_WIDTH = 3
NORM_EPS = 1e-6

LANES = 128
HEAD_PAD = LANES
MIX_IN_PAD = POOL_DIM + Q_LORA_RANK + KV_LORA_RANK + HEAD_PAD
POOL_HALO = 16
CONV_HALO = 8
MASK_VALUE = -1e30
VMEM_LIMIT = 56 * 1024 * 1024

V_ROWS = 80
LOG2_E = 1.4426950408889634

TOKEN_TILE = 512
FFN_TILE = 512
FFN_CHUNKS = ((0, 1024), (1024, 2048), (2048, D_FF))
ATTN_Q_TILE = 1024
ATTN_K_TILE = 256
ATTN_BLOCKS_PER_Q = ATTN_Q_TILE // ATTN_K_TILE
ATTN_BLOCKS_PER_TRIP = 4 * ATTN_BLOCKS_PER_Q

F32 = jnp.float32
BF16 = jnp.bfloat16


def _rms(x, g, n=None):
    n = x.shape[-1] if n is None else n
    ms = jnp.sum(x * x, axis=-1, keepdims=True) * (1.0 / n)
    return x * lax.rsqrt(ms + NORM_EPS) * g


def _const_spec(shape):
    nd = len(shape)
    return pl.BlockSpec(shape, lambda *_: (0,) * nd, pipeline_mode=pl.Buffered(1))


def _layer_spec(stacked, layer):
    nd = stacked.ndim
    return pl.BlockSpec((None,) + stacked.shape[1:], lambda *_: (layer,) + (0,) * (nd - 1),
                        pipeline_mode=pl.Buffered(1))


def _rows(v):
    return v.reshape(v.shape[0], 1, v.shape[1])


def _params(sem):
    return pltpu.CompilerParams(dimension_semantics=sem, vmem_limit_bytes=VMEM_LIMIT)


def _ffn_half_step(x, g_ref, wg_ref, wu_ref, wd_ref):
    xn = _rms(x, g_ref[...]).astype(wg_ref.dtype)
    y = None
    for lo, hi in FFN_CHUNKS:
        gate = jnp.dot(xn, wg_ref[:, lo:hi], preferred_element_type=F32)
        up = jnp.dot(xn, wu_ref[:, lo:hi], preferred_element_type=F32)
        h = (gate * jax.nn.sigmoid(gate) * up).astype(wd_ref.dtype)
        part = jnp.dot(h, wd_ref[lo:hi, :], preferred_element_type=F32)
        y = part if y is None else y + part
    return x + 0.5 * y


def _ffn_body(x_ref, g_ref, wg_ref, wu_ref, wd_ref, o_ref):
    o_ref[...] = _ffn_half_step(x_ref[...], g_ref, wg_ref, wu_ref, wd_ref)


def _mix_ffn_body(x_ref, pool_ref, at_ref, wo_ref, g_ref, wg_ref, wu_ref, wd_ref, o_ref):
    wo_pool = wo_ref[:POOL_DIM, :].astype(BF16)
    wo_attn = wo_ref[POOL_DIM:, :].astype(BF16)
    y = jnp.dot(pool_ref[...], wo_pool, preferred_element_type=F32)
    y = y + lax.dot_general(at_ref[...], wo_attn, (((0,), (0,)), ((), ())),
                            preferred_element_type=F32)
    o_ref[...] = _ffn_half_step(x_ref[...] + y, g_ref, wg_ref, wu_ref, wd_ref)


def _ffn(x, layer, g, wg, wu, wd, mix=None):
    s, d = x.shape
    tm = FFN_TILE
    g = _rows(g)
    row = lambda i: (i, 0)
    ffn_specs = [_layer_spec(g, layer), _layer_spec(wg, layer), _layer_spec(wu, layer),
                 _layer_spec(wd, layer)]
    if mix is None:
        body, name = _ffn_body, "ffn"
        operands = (x, g, wg, wu, wd)
        in_specs = [pl.BlockSpec((tm, d), row)] + ffn_specs
    else:
        pool, attn_t, idx, wo = mix
        body, name = _mix_ffn_body, "mix_ffn"
        operands = (x, pool, attn_t, wo, g, wg, wu, wd)
        in_specs = [pl.BlockSpec((tm, d), row), pl.BlockSpec((tm, POOL_DIM), row),
                    pl.BlockSpec((attn_t.shape[0], tm), lambda i: (0, i)),
                    _layer_spec(wo, idx)] + ffn_specs
    return pl.pallas_call(
        body,
        grid=(s // tm,),
        in_specs=in_specs,
        out_specs=pl.BlockSpec((tm, d), row),
        out_shape=jax.ShapeDtypeStruct((s, d), F32),
        compiler_params=_params(("parallel",)),
        name=name,
    )(*operands)


def _gain_rope_t(xt, gain_col, cos_t, sin_t):
    half = QK_ROPE_DIM // 2
    a, b, c = QK_NOPE_DIM, QK_NOPE_DIM + half, QK_HEAD_DIM
    y = xt * gain_col
    x1, x2 = y[a:b], y[b:c]
    return jnp.concatenate([y[:a], x1 * cos_t - x2 * sin_t, x2 * cos_t + x1 * sin_t, y[c:]], axis=0)


def _mla_pre_body(x_ref, g_ref, w_in_ref, qa_ref, wq_ref, kva_ref, wk_ref, wv_ref,
                  qh_col_ref, kh_ref, kh_col_ref, wp_ref, ps_ref, cos_ref, sin_ref,
                  pool_ref, qt_ref, k_ref, vt_ref, tail_ref):
    i = pl.program_id(0)
    tm = x_ref.shape[0]

    @pl.when(i == 0)
    def _():
        tail_ref[...] = jnp.zeros_like(tail_ref)

    hn = _rms(x_ref[...], g_ref[...]).astype(BF16)
    z = jnp.dot(hn, w_in_ref[...], preferred_element_type=F32)
    c1 = POOL_DIM
    c2 = c1 + Q_LORA_RANK
    c3 = c2 + KV_LORA_RANK
    u = z[:, :c1]
    q_lat = z[:, c1:c2]
    kv_lat = z[:, c2:c3]
    k_rope = z[:, c3:]

    ue = jnp.concatenate([tail_ref[...], u], axis=0)
    tail_ref[...] = u[tm - POOL_HALO:, :]
    pos = i * tm + lax.broadcasted_iota(jnp.int32, (tm, 1), 0)
    pooled_out = []
    for g, w in enumerate(POOL_WINDOWS):
        a = ue[:, g * POOL_GROUP_DIM:(g + 1) * POOL_GROUP_DIM]
        shift = 1
        while shift < w:
            a = a + pltpu.roll(a, shift, axis=0)
            shift *= 2
        wsum = a[POOL_HALO:, :]
        cnt = jnp.minimum(pos + 1, w).astype(F32)
        pooled = wsum / cnt - u[:, g * POOL_GROUP_DIM:(g + 1) * POOL_GROUP_DIM]
        pooled_out.append(jnp.dot(pooled.astype(BF16), wp_ref[g], preferred_element_type=F32))
    pool_ref[...] = (jnp.concatenate(pooled_out, axis=-1) * ps_ref[...]).astype(BF16)

    cos_t = cos_ref[...]
    sin_t = sin_ref[...]

    qn = _rms(q_lat, qa_ref[...]).astype(BF16)
    q = jnp.dot(qn, wq_ref[...], preferred_element_type=F32)
    scale = QK_HEAD_DIM ** -0.5 * LOG2_E
    for h in range(MLA_HEADS):
        qt = q[:, h * HEAD_PAD:(h + 1) * HEAD_PAD].T
        ms = jnp.sum(qt * qt, axis=0, keepdims=True) * (1.0 / QK_HEAD_DIM)
        r = lax.rsqrt(ms + NORM_EPS) * scale
        qt_ref[h] = (_gain_rope_t(qt, qh_col_ref[...], cos_t, sin_t) * r).astype(BF16)

    kvn = _rms(kv_lat, kva_ref[...]).astype(BF16)
    kk = jnp.dot(kvn, wk_ref[...], preferred_element_type=F32)
    vv = jnp.dot(kvn, wv_ref[...], preferred_element_type=F32)
    ones_lane = (lax.broadcasted_iota(jnp.int32, (1, HEAD_PAD), 1) == V_HEAD_DIM).astype(F32)
    gk = kh_ref[...]
    rope_rot = _gain_rope_t(k_rope.T, kh_col_ref[...], cos_t, sin_t).T
    rope_ssq = jnp.sum(k_rope * k_rope, axis=-1, keepdims=True)
    for h in range(MLA_HEADS):
        kn = kk[:, h * HEAD_PAD:(h + 1) * HEAD_PAD]
        ms = (jnp.sum(kn * kn, axis=-1, keepdims=True) + rope_ssq) * (1.0 / QK_HEAD_DIM)
        k_ref[h] = ((kn * gk + rope_rot) * lax.rsqrt(ms + NORM_EPS)).astype(BF16)
        vh = vv[:, h * HEAD_PAD:(h + 1) * HEAD_PAD] + ones_lane
        vt_ref[h] = vh.T[:V_ROWS].astype(BF16)


def _mla_pre(x, g, w_in, qa, wq, kva, wk, wv, qh, kh, wp, ps, cos_t, sin_t):
    s, d = x.shape
    tm = TOKEN_TILE
    row = lambda i: (i, 0)
    half = QK_ROPE_DIM // 2
    return pl.pallas_call(
        _mla_pre_body,
        grid=(s // tm,),
        in_specs=[
            pl.BlockSpec((tm, d), row),
            _const_spec((1, d)),
            _const_spec(w_in.shape),
            _const_spec((1, Q_LORA_RANK)),
            _const_spec(wq.shape),
            _const_spec((1, KV_LORA_RANK)),
            _const_spec(wk.shape),
            _const_spec(wv.shape),
            _const_spec((HEAD_PAD, 1)),
            _const_spec((1, HEAD_PAD)),
            _const_spec((HEAD_PAD, 1)),
            _const_spec(wp.shape),
            _const_spec((1, POOL_DIM)),
            pl.BlockSpec((half, tm), lambda i: (0, i)),
            pl.BlockSpec((half, tm), lambda i: (0, i)),
        ],
        out_specs=[
            pl.BlockSpec((tm, POOL_DIM), row),
            pl.BlockSpec((MLA_HEADS, HEAD_PAD, tm), lambda i: (0, 0, i)),
            pl.BlockSpec((MLA_HEADS, tm, HEAD_PAD), lambda i: (0, i, 0)),
            pl.BlockSpec((MLA_HEADS, V_ROWS, tm), lambda i: (0, 0, i)),
        ],
        out_shape=[
            jax.ShapeDtypeStruct((s, POOL_DIM), BF16),
            jax.ShapeDtypeStruct((MLA_HEADS, HEAD_PAD, s), BF16),
            jax.ShapeDtypeStruct((MLA_HEADS, s, HEAD_PAD), BF16),
            jax.ShapeDtypeStruct((MLA_HEADS, V_ROWS, s), BF16),
        ],
        scratch_shapes=[pltpu.VMEM((POOL_HALO, POOL_DIM), F32)],
        compiler_params=_params(("arbitrary",)),
        name="mla_pre",
    )(x, g.reshape(1, d), w_in, qa.reshape(1, -1), wq, kva.reshape(1, -1), wk, wv,
      qh.reshape(HEAD_PAD, 1), kh, kh.reshape(HEAD_PAD, 1), wp, ps.reshape(1, -1), cos_t, sin_t)


def _flash_body(qt_ref, qt_next_ref, k_ref, vt_ref, o_ref, s_refs, bm_refs, m_ref, acc_ref):
    i = pl.program_id(1)
    tq = qt_ref.shape[1]
    tk = ATTN_K_TILE
    m_ref[...] = jnp.full_like(m_ref, MASK_VALUE)
    acc_ref[...] = jnp.zeros_like(acc_ref)

    def scores(j, buf, c0=0, q_ref=qt_ref):
        off = pl.multiple_of(j * tk, tk)
        st = jnp.dot(k_ref[pl.ds(off, tk), :], q_ref[:, c0:], preferred_element_type=F32)
        s_refs[buf][:, c0:] = st
        bm_refs[buf][:, c0:] = jnp.max(st, axis=0, keepdims=True)

    def consume(j, buf, c0=0, diagonal=False):
        off = pl.multiple_of(j * tk, tk)
        st = s_refs[buf][:, c0:]
        if diagonal:
            row = lax.broadcasted_iota(jnp.int32, (tk, tk), 0)
            col = lax.broadcasted_iota(jnp.int32, (tk, tk), 1)
            square = jnp.where(row <= col, st[:, :tk], MASK_VALUE)
            st = square if tq - c0 == tk else jnp.concatenate([square, st[:, tk:]], axis=1)
            block_max = jnp.max(st, axis=0, keepdims=True)
        else:
            block_max = bm_refs[buf][:, c0:]
        m_old = m_ref[:, c0:]
        m_new = jnp.maximum(m_old, block_max)
        p = jnp.exp2(st - m_new).astype(BF16)
        alpha = jnp.exp2(m_old - m_new)
        vt = vt_ref[:, pl.ds(off, tk)]
        acc_ref[:, c0:] = acc_ref[:, c0:] * alpha + jnp.dot(vt, p, preferred_element_type=F32)
        m_ref[:, c0:] = m_new

    nb = ATTN_BLOCKS_PER_Q
    pairs = ((0, 1), (2, 3))

    @pl.when(i == 0)
    def _():
        scores(0, 0)
        scores(1, 1)

    def run_blocks(j, count):
        for ph in range(count // 2):
            drain, fill = pairs[ph % 2], pairs[(ph + 1) % 2]
            b = j + 2 * ph
            scores(b + 2, fill[0])
            consume(b, drain[0])
            scores(b + 3, fill[1])
            consume(b + 1, drain[1])

    full = nb * i
    trips = full // ATTN_BLOCKS_PER_TRIP

    def body(jj, carry):
        run_blocks(ATTN_BLOCKS_PER_TRIP * jj, ATTN_BLOCKS_PER_TRIP)
        return carry

    lax.fori_loop(0, trips, body, 0)

    def diagonal_blocks(j):
        for ph in range(nb // 2):
            drain, fill = pairs[ph % 2], pairs[(ph + 1) % 2]
            d = 2 * ph
            last = d + 2 >= nb
            frees_first_pair = d + 4 >= nb and not last
            if not last:
                scores(j + d + 2, fill[0], c0=(d + 2) * tk)
            consume(j + d, drain[0], c0=d * tk, diagonal=True)
            if frees_first_pair:
                scores(0, drain[0], q_ref=qt_next_ref)
            if not last:
                scores(j + d + 3, fill[1], c0=(d + 3) * tk)
            consume(j + d + 1, drain[1], c0=(d + 1) * tk, diagonal=True)
            if frees_first_pair:
                scores(1, drain[1], q_ref=qt_next_ref)
        acc = acc_ref[...]
        o_ref[...] = (acc[:V_HEAD_DIM] / acc[V_HEAD_DIM:V_HEAD_DIM + 1]).astype(BF16)

    for rest in range(0, ATTN_BLOCKS_PER_TRIP, nb):
        @pl.when(full - trips * ATTN_BLOCKS_PER_TRIP == rest)
        def _():
            run_blocks(trips * ATTN_BLOCKS_PER_TRIP, rest)
            diagonal_blocks(full)


def _flash(qt, k, vt):
    h, s, dp = k.shape
    tq, tk = ATTN_Q_TILE, ATTN_K_TILE
    assert tq == ATTN_BLOCKS_PER_Q * tk and ATTN_BLOCKS_PER_Q % 4 == 0
    n_buf = 4
    n_q = s // tq
    return pl.pallas_call(
        _flash_body,
        grid=(h, n_q),
        in_specs=[
            pl.BlockSpec((None, dp, tq), lambda hh, i: (hh, 0, i)),
            pl.BlockSpec((None, dp, tq), lambda hh, i: (hh, 0, jnp.minimum(i + 1, n_q - 1))),
            pl.BlockSpec((None, s, dp), lambda hh, i: (hh, 0, 0)),
            pl.BlockSpec((None, V_ROWS, s), lambda hh, i: (hh, 0, 0)),
        ],
        out_specs=pl.BlockSpec((V_HEAD_DIM, tq), lambda hh, i: (hh, i)),
        out_shape=jax.ShapeDtypeStruct((h * V_HEAD_DIM, s), BF16),
        scratch_shapes=[[pltpu.VMEM((tk, tq), F32)] * n_buf, [pltpu.VMEM((1, tq), F32)] * n_buf,
                        pltpu.VMEM((1, tq), F32), pltpu.VMEM((V_ROWS, tq), F32)],
        compiler_params=_params(("arbitrary", "arbitrary")),
        name="flash",
    )(qt, qt, k, vt)


def _conv_body(x_ref, g_ref, w_in_ref, cw_ref, w_out_ref, o_ref, tail_ref):
    i = pl.program_id(0)
    tm, d = x_ref.shape

    @pl.when(i == 0)
    def _():
        tail_ref[...] = jnp.zeros_like(tail_ref)

    x = x_ref[...]
    hn = _rms(x, g_ref[...]).astype(w_in_ref.dtype)
    z = jnp.dot(hn, w_in_ref[...], preferred_element_type=F32)
    gb = z[:, :d]
    u = z[:, d:2 * d] * z[:, 2 * d:]
    ue = jnp.concatenate([tail_ref[...], u], axis=0)
    tail_ref[...] = u[tm - CONV_HALO:, :]
    cw = cw_ref[...]
    y = cw[CONV_WIDTH - 1:CONV_WIDTH] * u
    for back in range(1, CONV_WIDTH):
        tap = CONV_WIDTH - 1 - back
        y = y + cw[tap:tap + 1] * pltpu.roll(ue, back, axis=0)[CONV_HALO:, :]
    y = (gb * y).astype(w_out_ref.dtype)
    o_ref[...] = x + jnp.dot(y, w_out_ref[...], preferred_element_type=F32)


def _conv_mixer(x, layer, g, idx, w_in, cw, w_out):
    s, d = x.shape
    tm = TOKEN_TILE
    row = lambda i: (i, 0)
    g = _rows(g)
    return pl.pallas_call(
        _conv_body,
        grid=(s // tm,),
        in_specs=[
            pl.BlockSpec((tm, d), row),
            _layer_spec(g, layer),
            _layer_spec(w_in, idx),
            _layer_spec(cw, idx),
            _layer_spec(w_out, idx),
        ],
        out_specs=pl.BlockSpec((tm, d), row),
        out_shape=jax.ShapeDtypeStruct((s, d), F32),
        scratch_shapes=[pltpu.VMEM((CONV_HALO, d), F32)],
        compiler_params=_params(("arbitrary",)),
        name="conv_mixer",
    )(x, g, w_in, cw, w_out)


def _pad_heads(w, head_dim, take_from, take_n):
    rows = w.shape[0]
    w = w.reshape(rows, MLA_HEADS, head_dim)[:, :, take_from:take_from + take_n]
    w = jnp.pad(w, ((0, 0), (0, 0), (0, HEAD_PAD - take_n)))
    return w.reshape(rows, MLA_HEADS * HEAD_PAD)


def _rope_tables(s):
    pos = jnp.arange(s, dtype=F32)
    inv_freq = ROPE_THETA ** (-jnp.arange(0, QK_ROPE_DIM, 2, dtype=F32) / QK_ROPE_DIM)
    ang = inv_freq[:, None] * pos[None, :]
    return jnp.cos(ang), jnp.sin(ang)


def kernel(x, ffn1_norm, ffn1_w_gate, ffn1_w_up, ffn1_w_down, mix_norm, ffn2_norm, ffn2_w_gate, ffn2_w_up, ffn2_w_down, a_w_in, a_q_a_norm, a_w_q_up, a_kv_a_norm, a_w_kv_up, a_q_head_norm, a_k_head_norm, a_w_pool, a_pool_scale, a_w_out, c_w_in, c_conv_w, c_w_out):
    b, s, d = x.shape
    assert b == 1 and d == D_MODEL and s % TOKEN_TILE == 0 and s % ATTN_Q_TILE == 0
    h = x.reshape(s, d)
    cos_t, sin_t = _rope_tables(s)
    head_pad = ((0, 0), (0, HEAD_PAD - QK_HEAD_DIM))

    for layer in range(DEPTH):
        h = _ffn(h, layer, ffn1_norm, ffn1_w_gate, ffn1_w_up, ffn1_w_down)
        i = layer // 2
        if layer % 2 == 0:
            w_in = a_w_in[i]
            c3 = POOL_DIM + Q_LORA_RANK + KV_LORA_RANK
            w_in_p = jnp.concatenate(
                [w_in[:, :c3], jnp.zeros((d, QK_NOPE_DIM), F32), w_in[:, c3:],
                 jnp.zeros((d, HEAD_PAD - QK_HEAD_DIM), F32)], axis=-1).astype(BF16)
            wq = _pad_heads(a_w_q_up[i], QK_HEAD_DIM, 0, QK_HEAD_DIM).astype(BF16)
            wk = _pad_heads(a_w_kv_up[i], QK_NOPE_DIM + V_HEAD_DIM, 0, QK_NOPE_DIM).astype(BF16)
            wv = _pad_heads(a_w_kv_up[i], QK_NOPE_DIM + V_HEAD_DIM, QK_NOPE_DIM, V_HEAD_DIM).astype(BF16)
            qh = jnp.pad(a_q_head_norm[i].reshape(1, -1), head_pad)
            kh = jnp.pad(a_k_head_norm[i].reshape(1, -1), head_pad)
            pool, qt, k, vt = _mla_pre(h, mix_norm[layer], w_in_p, a_q_a_norm[i], wq, a_kv_a_norm[i],
                                      wk, wv, qh, kh, a_w_pool[i].astype(BF16), a_pool_scale[i],
                                      cos_t, sin_t)
            attn_t = _flash(qt, k, vt)
            mix = (pool, attn_t, i, a_w_out)
        else:
            h = _conv_mixer(h, layer, mix_norm, i, c_w_in, c_conv_w, c_w_out)
            mix = None
        h = _ffn(h, layer, ffn2_norm, ffn2_w_gate, ffn2_w_up, ffn2_w_down, mix=mix)
    return h.reshape(b, s, d)
```

```python
import functools

import jax
import jax.numpy as jnp
from jax import lax
from jax.experimental import pallas as pl
from jax.experimental.pallas import tpu as pltpu

D_MODEL = 1024
DEPTH = 4
D_FF = 2816
POOL_WINDOWS = (2, 4, 8, 16)
POOL_DIM = 512
POOL_GROUP_DIM = 128
MLA_HEADS = 8
QK_NOPE_DIM = 64
QK_ROPE_DIM = 32
QK_HEAD_DIM = 96
V_HEAD_DIM = 64
Q_LORA_RANK = 384
KV_LORA_RANK = 256
ROPE_THETA = 10000.0
CONV_WIDTH = 3
NORM_EPS = 1e-6

LANES = 128
HEAD_PAD = LANES
MIX_IN_PAD = POOL_DIM + Q_LORA_RANK + KV_LORA_RANK + HEAD_PAD
POOL_HALO = 16
CONV_HALO = 8
MASK_VALUE = -1e30
VMEM_LIMIT = 56 * 1024 * 1024

V_ROWS = 80
LOG2_E = 1.4426950408889634

TOKEN_TILE = 512
FFN_TILE = 1024
FFN_CAST_SLABS = 16
FFN_CHUNKS = ((0, 1024), (1024, 2048), (2048, D_FF))
ATTN_Q_TILE = 1024
ATTN_K_TILE = 256
ATTN_BLOCKS_PER_Q = ATTN_Q_TILE // ATTN_K_TILE
ATTN_BLOCKS_PER_TRIP = 4 * ATTN_BLOCKS_PER_Q

F32 = jnp.float32
BF16 = jnp.bfloat16


def _rms(x, g, n=None):
    n = x.shape[-1] if n is None else n
    ms = jnp.sum(x * x, axis=-1, keepdims=True) * (1.0 / n)
    return x * lax.rsqrt(ms + NORM_EPS) * g


def _const_spec(shape):
    nd = len(shape)
    return pl.BlockSpec(shape, lambda *_: (0,) * nd, pipeline_mode=pl.Buffered(1))


def _layer_spec(stacked, layer):
    nd = stacked.ndim
    return pl.BlockSpec((None,) + stacked.shape[1:], lambda *_: (layer,) + (0,) * (nd - 1),
                        pipeline_mode=pl.Buffered(1))


def _rows(v):
    return v.reshape(v.shape[0], 1, v.shape[1])


def _params(sem):
    return pltpu.CompilerParams(dimension_semantics=sem, vmem_limit_bytes=VMEM_LIMIT)


def _ffn_half_step(x, g_ref, wg_ref, wu_ref, wd_ref):
    xn = _rms(x, g_ref[...]).astype(wg_ref.dtype)
    y = None
    for lo, hi in FFN_CHUNKS:
        gate = jnp.dot(xn, wg_ref[:, lo:hi], preferred_element_type=F32)
        up = jnp.dot(xn, wu_ref[:, lo:hi], preferred_element_type=F32)
        h = (gate * jax.nn.sigmoid(gate) * up).astype(wd_ref.dtype)
        part = jnp.dot(h, wd_ref[lo:hi, :], preferred_element_type=F32)
        y = part if y is None else y + part
    return x + 0.5 * y


def _ffn_body(has_mix, has_cast, *refs):
    refs = list(refs)
    x = refs.pop(0)[...]
    if has_mix:
        pool_ref, at_ref, wo_ref = refs[:3]
        del refs[:3]
        wo_pool = wo_ref[:POOL_DIM, :].astype(BF16)
        wo_attn = wo_ref[POOL_DIM:, :].astype(BF16)
        x = x + jnp.dot(pool_ref[...], wo_pool, preferred_element_type=F32)
        x = x + lax.dot_general(at_ref[...], wo_attn, (((0,), (0,)), ((), ())),
                                preferred_element_type=F32)
    g_ref, wg_ref, wu_ref, wd_ref = refs[:4]
    del refs[:4]
    src = refs[:3] if has_cast else []
    o_ref = refs[len(src)]
    o_ref[...] = _ffn_half_step(x, g_ref, wg_ref, wu_ref, wd_ref)
    for s_ref, d_ref in zip(src, refs[len(src) + 1:]):
        d_ref[...] = s_ref[...].astype(BF16)


def _ffn(x, g_row, weights, mix=None, cast_next=None):
    s, d = x.shape
    tm = FFN_TILE if mix is None else TOKEN_TILE
    steps = s // tm
    row = lambda i: (i, 0)
    operands = [x]
    in_specs = [pl.BlockSpec((tm, d), row)]
    if mix is not None:
        pool, attn_t, idx, wo = mix
        operands += [pool, attn_t, wo]
        in_specs += [pl.BlockSpec((tm, POOL_DIM), row),
                     pl.BlockSpec((attn_t.shape[0], tm), lambda i: (0, i)), _layer_spec(wo, idx)]
    operands += [g_row, *weights]
    in_specs += [_const_spec(g_row.shape)] + [_const_spec(w.shape) for w in weights]
    out_specs = [pl.BlockSpec((tm, d), row)]
    out_shape = [jax.ShapeDtypeStruct((s, d), F32)]
    if cast_next is not None:
        layer, *stacks = cast_next
        per_slab = steps // FFN_CAST_SLABS
        for w in stacks:
            rows = w.shape[1] // FFN_CAST_SLABS
            operands.append(w)
            in_specs.append(pl.BlockSpec((None, rows, w.shape[2]),
                                         lambda i: (layer, i // per_slab, 0)))
            out_specs.append(pl.BlockSpec((rows, w.shape[2]), lambda i: (i // per_slab, 0)))
            out_shape.append(jax.ShapeDtypeStruct(w.shape[1:], BF16))
    outs = pl.pallas_call(
        functools.partial(_ffn_body, mix is not None, cast_next is not None),
        grid=(steps,),
        in_specs=in_specs,
        out_specs=out_specs,
        out_shape=out_shape,
        compiler_params=_params(("arbitrary",)),
        name="ffn" if mix is None else "mix_ffn",
    )(*operands)
    return outs[0], tuple(outs[1:])


def _gain_rope_t(xt, gain_col, cos_t, sin_t):
    half = QK_ROPE_DIM // 2
    a, b, c = QK_NOPE_DIM, QK_NOPE_DIM + half, QK_HEAD_DIM
    y = xt * gain_col
    x1, x2 = y[a:b], y[b:c]
    return jnp.concatenate([y[:a], x1 * cos_t - x2 * sin_t, x2 * cos_t + x1 * sin_t, y[c:]], axis=0)


def _mla_pre_body(x_ref, g_ref, w_in_ref, qa_ref, wq_ref, kva_ref, wk_ref, wv_ref,
                  qh_col_ref, kh_ref, kh_col_ref, wp_ref, ps_ref, cos_ref, sin_ref,
                  pool_ref, qt_ref, k_ref, vt_ref, tail_ref):
    i = pl.program_id(0)
    tm = x_ref.shape[0]

    @pl.when(i == 0)
    def _():
        tail_ref[...] = jnp.zeros_like(tail_ref)

    hn = _rms(x_ref[...], g_ref[...]).astype(BF16)
    z = jnp.dot(hn, w_in_ref[...], preferred_element_type=F32)
    c1 = POOL_DIM
    c2 = c1 + Q_LORA_RANK
    c3 = c2 + KV_LORA_RANK
    u = z[:, :c1]
    q_lat = z[:, c1:c2]
    kv_lat = z[:, c2:c3]
    k_rope = z[:, c3:]

    ue = jnp.concatenate([tail_ref[...], u], axis=0)
    tail_ref[...] = u[tm - POOL_HALO:, :]
    pos = i * tm + lax.broadcasted_iota(jnp.int32, (tm, 1), 0)
    pooled_out = []
    for g, w in enumerate(POOL_WINDOWS):
        a = ue[:, g * POOL_GROUP_DIM:(g + 1) * POOL_GROUP_DIM]
        shift = 1
        while shift < w:
            a = a + pltpu.roll(a, shift, axis=0)
            shift *= 2
        wsum = a[POOL_HALO:, :]
        cnt = jnp.minimum(pos + 1, w).astype(F32)
        pooled = wsum / cnt - u[:, g * POOL_GROUP_DIM:(g + 1) * POOL_GROUP_DIM]
        pooled_out.append(jnp.dot(pooled.astype(BF16), wp_ref[g], preferred_element_type=F32))
    pool_ref[...] = (jnp.concatenate(pooled_out, axis=-1) * ps_ref[...]).astype(BF16)

    cos_t = cos_ref[...]
    sin_t = sin_ref[...]

    qn = _rms(q_lat, qa_ref[...]).astype(BF16)
    q = jnp.dot(qn, wq_ref[...], preferred_element_type=F32)
    scale = QK_HEAD_DIM ** -0.5 * LOG2_E
    for h in range(MLA_HEADS):
        qt = q[:, h * HEAD_PAD:(h + 1) * HEAD_PAD].T
        ms = jnp.sum(qt * qt, axis=0, keepdims=True) * (1.0 / QK_HEAD_DIM)
        r = lax.rsqrt(ms + NORM_EPS) * scale
        qt_ref[h] = (_gain_rope_t(qt, qh_col_ref[...], cos_t, sin_t) * r).astype(BF16)

    kvn = _rms(kv_lat, kva_ref[...]).astype(BF16)
    kk = jnp.dot(kvn, wk_ref[...], preferred_element_type=F32)
    vv = jnp.dot(kvn, wv_ref[...], preferred_element_type=F32)
    ones_lane = (lax.broadcasted_iota(jnp.int32, (1, HEAD_PAD), 1) == V_HEAD_DIM).astype(F32)
    gk = kh_ref[...]
    rope_rot = _gain_rope_t(k_rope.T, kh_col_ref[...], cos_t, sin_t).T
    rope_ssq = jnp.sum(k_rope * k_rope, axis=-1, keepdims=True)
    for h in range(MLA_HEADS):
        kn = kk[:, h * HEAD_PAD:(h + 1) * HEAD_PAD]
        ms = (jnp.sum(kn * kn, axis=-1, keepdims=True) + rope_ssq) * (1.0 / QK_HEAD_DIM)
        k_ref[h] = ((kn * gk + rope_rot) * lax.rsqrt(ms + NORM_EPS)).astype(BF16)
        vh = vv[:, h * HEAD_PAD:(h + 1) * HEAD_PAD] + ones_lane
        vt_ref[h] = vh.T[:V_ROWS].astype(BF16)


def _mla_pre(x, g, w_in, qa, wq, kva, wk, wv, qh, kh, wp, ps, cos_t, sin_t):
    s, d = x.shape
    tm = TOKEN_TILE
    row = lambda i: (i, 0)
    half = QK_ROPE_DIM // 2
    return pl.pallas_call(
        _mla_pre_body,
        grid=(s // tm,),
        in_specs=[
            pl.BlockSpec((tm, d), row),
            _const_spec((1, d)),
            _const_spec(w_in.shape),
            _const_spec((1, Q_LORA_RANK)),
            _const_spec(wq.shape),
            _const_spec((1, KV_LORA_RANK)),
            _const_spec(wk.shape),
            _const_spec(wv.shape),
            _const_spec((HEAD_PAD, 1)),
            _const_spec((1, HEAD_PAD)),
            _const_spec((HEAD_PAD, 1)),
            _const_spec(wp.shape),
            _const_spec((1, POOL_DIM)),
            pl.BlockSpec((half, tm), lambda i: (0, i)),
            pl.BlockSpec((half, tm), lambda i: (0, i)),
        ],
        out_specs=[
            pl.BlockSpec((tm, POOL_DIM), row),
            pl.BlockSpec((MLA_HEADS, HEAD_PAD, tm), lambda i: (0, 0, i)),
            pl.BlockSpec((MLA_HEADS, tm, HEAD_PAD), lambda i: (0, i, 0)),
            pl.BlockSpec((MLA_HEADS, V_ROWS, tm), lambda i: (0, 0, i)),
        ],
        out_shape=[
            jax.ShapeDtypeStruct((s, POOL_DIM), BF16),
            jax.ShapeDtypeStruct((MLA_HEADS, HEAD_PAD, s), BF16),
            jax.ShapeDtypeStruct((MLA_HEADS, s, HEAD_PAD), BF16),
            jax.ShapeDtypeStruct((MLA_HEADS, V_ROWS, s), BF16),
        ],
        scratch_shapes=[pltpu.VMEM((POOL_HALO, POOL_DIM), F32)],
        compiler_params=_params(("arbitrary",)),
        name="mla_pre",
    )(x, g.reshape(1, d), w_in, qa.reshape(1, -1), wq, kva.reshape(1, -1), wk, wv,
      qh.reshape(HEAD_PAD, 1), kh, kh.reshape(HEAD_PAD, 1), wp, ps.reshape(1, -1), cos_t, sin_t)


def _flash_body(qt_ref, qt_next_ref, k_ref, vt_ref, o_ref, s_refs, bm_refs, m_ref, acc_ref):
    i = pl.program_id(1)
    tq = qt_ref.shape[1]
    tk = ATTN_K_TILE
    m_ref[...] = jnp.full_like(m_ref, MASK_VALUE)
    acc_ref[...] = jnp.zeros_like(acc_ref)

    def scores(j, buf, c0=0, q_ref=qt_ref):
        off = pl.multiple_of(j * tk, tk)
        st = jnp.dot(k_ref[pl.ds(off, tk), :], q_ref[:, c0:], preferred_element_type=F32)
        s_refs[buf][:, c0:] = st
        bm_refs[buf][:, c0:] = jnp.max(st, axis=0, keepdims=True)

    def consume(j, buf, c0=0, diagonal=False):
        off = pl.multiple_of(j * tk, tk)
        st = s_refs[buf][:, c0:]
        if diagonal:
            row = lax.broadcasted_iota(jnp.int32, (tk, tk), 0)
            col = lax.broadcasted_iota(jnp.int32, (tk, tk), 1)
            square = jnp.where(row <= col, st[:, :tk], MASK_VALUE)
            st = square if tq - c0 == tk else jnp.concatenate([square, st[:, tk:]], axis=1)
            block_max = jnp.max(st, axis=0, keepdims=True)
        else:
            block_max = bm_refs[buf][:, c0:]
        m_old = m_ref[:, c0:]
        m_new = jnp.maximum(m_old, block_max)
        p = jnp.exp2(st - m_new).astype(BF16)
        alpha = jnp.exp2(m_old - m_new)
        vt = vt_ref[:, pl.ds(off, tk)]
        acc_ref[:, c0:] = acc_ref[:, c0:] * alpha + jnp.dot(vt, p, preferred_element_type=F32)
        m_ref[:, c0:] = m_new

    nb = ATTN_BLOCKS_PER_Q
    pairs = ((0, 1), (2, 3))

    @pl.when(i == 0)
    def _():
        scores(0, 0)
        scores(1, 1)

    def run_blocks(j, count):
        for ph in range(count // 2):
            drain, fill = pairs[ph % 2], pairs[(ph + 1) % 2]
            b = j + 2 * ph
            scores(b + 2, fill[0])
            consume(b, drain[0])
            scores(b + 3, fill[1])
            consume(b + 1, drain[1])

    full = nb * i
    trips = full // ATTN_BLOCKS_PER_TRIP

    def body(jj, carry):
        run_blocks(ATTN_BLOCKS_PER_TRIP * jj, ATTN_BLOCKS_PER_TRIP)
        return carry

    lax.fori_loop(0, trips, body, 0)

    def diagonal_blocks(j):
        for ph in range(nb // 2):
            drain, fill = pairs[ph % 2], pairs[(ph + 1) % 2]
            d = 2 * ph
            last = d + 2 >= nb
            frees_first_pair = d + 4 >= nb and not last
            if not last:
                scores(j + d + 2, fill[0], c0=(d + 2) * tk)
            consume(j + d, drain[0], c0=d * tk, diagonal=True)
            if frees_first_pair:
                scores(0, drain[0], q_ref=qt_next_ref)
            if not last:
                scores(j + d + 3, fill[1], c0=(d + 3) * tk)
            consume(j + d + 1, drain[1], c0=(d + 1) * tk, diagonal=True)
            if frees_first_pair:
                scores(1, drain[1], q_ref=qt_next_ref)
        acc = acc_ref[...]
        o_ref[...] = (acc[:V_HEAD_DIM] / acc[V_HEAD_DIM:V_HEAD_DIM + 1]).astype(BF16)

    for rest in range(0, ATTN_BLOCKS_PER_TRIP, nb):
        @pl.when(full - trips * ATTN_BLOCKS_PER_TRIP == rest)
        def _():
            run_blocks(trips * ATTN_BLOCKS_PER_TRIP, rest)
            diagonal_blocks(full)


def _flash(qt, k, vt):
    h, s, dp = k.shape
    tq, tk = ATTN_Q_TILE, ATTN_K_TILE
    assert tq == ATTN_BLOCKS_PER_Q * tk and ATTN_BLOCKS_PER_Q % 4 == 0
    n_buf = 4
    n_q = s // tq
    return pl.pallas_call(
        _flash_body,
        grid=(h, n_q),
        in_specs=[
            pl.BlockSpec((None, dp, tq), lambda hh, i: (hh, 0, i)),
            pl.BlockSpec((None, dp, tq), lambda hh, i: (hh, 0, jnp.minimum(i + 1, n_q - 1))),
            pl.BlockSpec((None, s, dp), lambda hh, i: (hh, 0, 0)),
            pl.BlockSpec((None, V_ROWS, s), lambda hh, i: (hh, 0, 0)),
        ],
        out_specs=pl.BlockSpec((V_HEAD_DIM, tq), lambda hh, i: (hh, i)),
        out_shape=jax.ShapeDtypeStruct((h * V_HEAD_DIM, s), BF16),
        scratch_shapes=[[pltpu.VMEM((tk, tq), F32)] * n_buf, [pltpu.VMEM((1, tq), F32)] * n_buf,
                        pltpu.VMEM((1, tq), F32), pltpu.VMEM((V_ROWS, tq), F32)],
        compiler_params=_params(("arbitrary", "arbitrary")),
        name="flash",
    )(qt, qt, k, vt)


def _conv_body(x_ref, g_ref, w_in_ref, cw_ref, w_out_ref, o_ref, tail_ref):
    i = pl.program_id(0)
    tm, d = x_ref.shape

    @pl.when(i == 0)
    def _():
        tail_ref[...] = jnp.zeros_like(tail_ref)

    x = x_ref[...]
    hn = _rms(x, g_ref[...]).astype(w_in_ref.dtype)
    z = jnp.dot(hn, w_in_ref[...], preferred_element_type=F32)
    gb = z[:, :d]
    u = z[:, d:2 * d] * z[:, 2 * d:]
    ue = jnp.concatenate([tail_ref[...], u], axis=0)
    tail_ref[...] = u[tm - CONV_HALO:, :]
    cw = cw_ref[...]
    y = cw[CONV_WIDTH - 1:CONV_WIDTH] * u
    for back in range(1, CONV_WIDTH):
        tap = CONV_WIDTH - 1 - back
        y = y + cw[tap:tap + 1] * pltpu.roll(ue, back, axis=0)[CONV_HALO:, :]
    y = (gb * y).astype(w_out_ref.dtype)
    o_ref[...] = x + jnp.dot(y, w_out_ref[...], preferred_element_type=F32)


def _conv_mixer(x, layer, g, idx, w_in, cw, w_out):
    s, d = x.shape
    tm = TOKEN_TILE
    row = lambda i: (i, 0)
    g = _rows(g)
    return pl.pallas_call(
        _conv_body,
        grid=(s // tm,),
        in_specs=[
            pl.BlockSpec((tm, d), row),
            _layer_spec(g, layer),
            _layer_spec(w_in, idx),
            _layer_spec(cw, idx),
            _layer_spec(w_out, idx),
        ],
        out_specs=pl.BlockSpec((tm, d), row),
        out_shape=jax.ShapeDtypeStruct((s, d), F32),
        scratch_shapes=[pltpu.VMEM((CONV_HALO, d), F32)],
        compiler_params=_params(("arbitrary",)),
        name="conv_mixer",
    )(x, g, w_in, cw, w_out)


def _pad_heads(w, head_dim, take_from, take_n):
    rows = w.shape[0]
    w = w.reshape(rows, MLA_HEADS, head_dim)[:, :, take_from:take_from + take_n]
    w = jnp.pad(w, ((0, 0), (0, 0), (0, HEAD_PAD - take_n)))
    return w.reshape(rows, MLA_HEADS * HEAD_PAD)


def _rope_tables(s):
    pos = jnp.arange(s, dtype=F32)
    inv_freq = ROPE_THETA ** (-jnp.arange(0, QK_ROPE_DIM, 2, dtype=F32) / QK_ROPE_DIM)
    ang = inv_freq[:, None] * pos[None, :]
    return jnp.cos(ang), jnp.sin(ang)


def kernel(x, ffn1_norm, ffn1_w_gate, ffn1_w_up, ffn1_w_down, mix_norm, ffn2_norm, ffn2_w_gate, ffn2_w_up, ffn2_w_down, a_w_in, a_q_a_norm, a_w_q_up, a_kv_a_norm, a_w_kv_up, a_q_head_norm, a_k_head_norm, a_w_pool, a_pool_scale, a_w_out, c_w_in, c_conv_w, c_w_out):
    b, s, d = x.shape
    assert b == 1 and d == D_MODEL and s % TOKEN_TILE == 0 and s % ATTN_Q_TILE == 0
    h = x.reshape(s, d)
    cos_t, sin_t = _rope_tables(s)
    head_pad = ((0, 0), (0, HEAD_PAD - QK_HEAD_DIM))

    ffn_stacks = []
    for layer in range(DEPTH):
        ffn_stacks.append((layer, ffn1_norm, ffn1_w_gate, ffn1_w_up, ffn1_w_down))
        ffn_stacks.append((layer, ffn2_norm, ffn2_w_gate, ffn2_w_up, ffn2_w_down))
    ffn_weights = tuple(w[0].astype(BF16) for w in ffn_stacks[0][2:])

    def ffn_step(h, n, ffn_weights, mix=None):
        layer, norm = ffn_stacks[n][:2]
        nxt = ffn_stacks[n + 1] if n + 1 < len(ffn_stacks) else None
        cast_next = None if nxt is None else (nxt[0],) + nxt[2:]
        return _ffn(h, norm[layer].reshape(1, d), ffn_weights, mix=mix, cast_next=cast_next)

    for layer in range(DEPTH):
        h, ffn_weights = ffn_step(h, 2 * layer, ffn_weights)
        i = layer // 2
        if layer % 2 == 0:
            w_in = a_w_in[i]
            c3 = POOL_DIM + Q_LORA_RANK + KV_LORA_RANK
            w_in_p = jnp.concatenate(
                [w_in[:, :c3], jnp.zeros((d, QK_NOPE_DIM), F32), w_in[:, c3:],
                 jnp.zeros((d, HEAD_PAD - QK_HEAD_DIM), F32)], axis=-1).astype(BF16)
            wq = _pad_heads(a_w_q_up[i], QK_HEAD_DIM, 0, QK_HEAD_DIM).astype(BF16)
            wk = _pad_heads(a_w_kv_up[i], QK_NOPE_DIM + V_HEAD_DIM, 0, QK_NOPE_DIM).astype(BF16)
            wv = _pad_heads(a_w_kv_up[i], QK_NOPE_DIM + V_HEAD_DIM, QK_NOPE_DIM, V_HEAD_DIM).astype(BF16)
            qh = jnp.pad(a_q_head_norm[i].reshape(1, -1), head_pad)
            kh = jnp.pad(a_k_head_norm[i].reshape(1, -1), head_pad)
            pool, qt, k, vt = _mla_pre(h, mix_norm[layer], w_in_p, a_q_a_norm[i], wq, a_kv_a_norm[i],
                                      wk, wv, qh, kh, a_w_pool[i].astype(BF16), a_pool_scale[i],
                                      cos_t, sin_t)
            attn_t = _flash(qt, k, vt)
            mix = (pool, attn_t, i, a_w_out)
        else:
            h = _conv_mixer(h, layer, mix_norm, i, c_w_in, c_conv_w, c_w_out)
            mix = None
        h, ffn_weights = ffn_step(h, 2 * layer + 1, ffn_weights, mix=mix)
    return h.reshape(b, s, d)
```

```python
import functools

import jax
import jax.numpy as jnp
from jax import lax
from jax.experimental import pallas as pl
from jax.experimental.pallas import tpu as pltpu

D_MODEL = 1024
DEPTH = 4
D_FF = 2816
POOL_WINDOWS = (2, 4, 8, 16)
POOL_DIM = 512
POOL_GROUP_DIM = 128
MLA_HEADS = 8
QK_NOPE_DIM = 64
QK_ROPE_DIM = 32
QK_HEAD_DIM = 96
V_HEAD_DIM = 64
Q_LORA_RANK = 384
KV_LORA_RANK = 256
ROPE_THETA = 10000.0
CONV_WIDTH = 3
NORM_EPS = 1e-6

LANES = 128
HEAD_PAD = LANES
MIX_IN_PAD = POOL_DIM + Q_LORA_RANK + KV_LORA_RANK + HEAD_PAD
POOL_HALO = 16
CONV_HALO = 8
MASK_VALUE = -1e30
VMEM_LIMIT = 56 * 1024 * 1024

V_ROWS = 80
LOG2_E = 1.4426950408889634

TOKEN_TILE = 512
MLA_PRE_TILE = 1024
CONV_TILE = 1024
FFN_TILE = 1024
FFN_CAST_SLABS = 16
FFN_CHUNKS = ((0, 1024), (1024, 2048), (2048, D_FF))
ATTN_Q_TILE = 1024
ATTN_K_TILE = 256
ATTN_BLOCKS_PER_Q = ATTN_Q_TILE // ATTN_K_TILE
ATTN_BLOCKS_PER_TRIP = 4 * ATTN_BLOCKS_PER_Q

F32 = jnp.float32
BF16 = jnp.bfloat16


def _rms(x, g, n=None):
    n = x.shape[-1] if n is None else n
    ms = jnp.sum(x * x, axis=-1, keepdims=True) * (1.0 / n)
    return x * lax.rsqrt(ms + NORM_EPS) * g


def _const_spec(shape):
    nd = len(shape)
    return pl.BlockSpec(shape, lambda *_: (0,) * nd, pipeline_mode=pl.Buffered(1))


def _layer_spec(stacked, layer):
    nd = stacked.ndim
    return pl.BlockSpec((None,) + stacked.shape[1:], lambda *_: (layer,) + (0,) * (nd - 1),
                        pipeline_mode=pl.Buffered(1))


def _rows(v):
    return v.reshape(v.shape[0], 1, v.shape[1])


def _params(sem):
    return pltpu.CompilerParams(dimension_semantics=sem, vmem_limit_bytes=VMEM_LIMIT)


def _ffn_half_step(x, g_ref, wg_ref, wu_ref, wd_ref):
    xn = _rms(x, g_ref[...]).astype(wg_ref.dtype)
    y = None
    for lo, hi in FFN_CHUNKS:
        gate = jnp.dot(xn, wg_ref[:, lo:hi], preferred_element_type=F32)
        up = jnp.dot(xn, wu_ref[:, lo:hi], preferred_element_type=F32)
        h = (gate * jax.nn.sigmoid(gate) * up).astype(wd_ref.dtype)
        part = jnp.dot(h, wd_ref[lo:hi, :], preferred_element_type=F32)
        y = part if y is None else y + part
    return x + 0.5 * y


def _ffn_body(has_mix, n_cast, *refs):
    refs = list(refs)
    x = refs.pop(0)[...]
    if has_mix:
        pool_ref, at_ref, wo_ref = refs[:3]
        del refs[:3]
        wo_pool = wo_ref[:POOL_DIM, :].astype(BF16)
        wo_attn = wo_ref[POOL_DIM:, :].astype(BF16)
        x = x + jnp.dot(pool_ref[...], wo_pool, preferred_element_type=F32)
        x = x + lax.dot_general(at_ref[...], wo_attn, (((0,), (0,)), ((), ())),
                                preferred_element_type=F32)
    g_ref, wg_ref, wu_ref, wd_ref = refs[:4]
    del refs[:4]
    src = refs[:n_cast]
    o_ref = refs[n_cast]
    o_ref[...] = _ffn_half_step(x, g_ref, wg_ref, wu_ref, wd_ref)
    for s_ref, d_ref in zip(src, refs[n_cast + 1:]):
        d_ref[...] = s_ref[...].astype(BF16)


def _ffn(x, g_row, weights, mix=None, cast_next=()):
    s, d = x.shape
    from_f32 = isinstance(weights[0], tuple)
    tm = FFN_TILE if mix is None and not from_f32 else TOKEN_TILE
    steps = s // tm
    row = lambda i: (i, 0)
    operands = [x]
    in_specs = [pl.BlockSpec((tm, d), row)]
    if mix is not None:
        pool, attn_t, idx, wo = mix
        operands += [pool, attn_t, wo]
        in_specs += [pl.BlockSpec((tm, POOL_DIM), row),
                     pl.BlockSpec((attn_t.shape[0], tm), lambda i: (0, i)), _layer_spec(wo, idx)]
    operands.append(g_row)
    in_specs.append(_const_spec(g_row.shape))
    for w in weights:
        operands.append(w[0] if from_f32 else w)
        in_specs.append(_layer_spec(*w) if from_f32 else _const_spec(w.shape))
    out_specs = [pl.BlockSpec((tm, d), row)]
    out_shape = [jax.ShapeDtypeStruct((s, d), F32)]
    per_slab = steps // FFN_CAST_SLABS
    for w, idx in cast_next:
        rows = w.shape[1] // FFN_CAST_SLABS
        operands.append(w)
        in_specs.append(pl.BlockSpec((None, rows, w.shape[2]),
                                     lambda i, idx=idx: (idx, i // per_slab, 0)))
        out_specs.append(pl.BlockSpec((rows, w.shape[2]), lambda i: (i // per_slab, 0)))
        out_shape.append(jax.ShapeDtypeStruct(w.shape[1:], BF16))
    outs = pl.pallas_call(
        functools.partial(_ffn_body, mix is not None, len(cast_next)),
        grid=(steps,),
        in_specs=in_specs,
        out_specs=out_specs,
        out_shape=out_shape,
        compiler_params=_params(("arbitrary",)),
        name="ffn" if mix is None else "mix_ffn",
    )(*operands)
    return outs[0], tuple(outs[1:])


def _gain_rope_t(xt, gain_col, cos_t, sin_t):
    half = QK_ROPE_DIM // 2
    a, b, c = QK_NOPE_DIM, QK_NOPE_DIM + half, QK_HEAD_DIM
    y = xt * gain_col
    x1, x2 = y[a:b], y[b:c]
    return jnp.concatenate([y[:a], x1 * cos_t - x2 * sin_t, x2 * cos_t + x1 * sin_t, y[c:]], axis=0)


def _mla_pre_body(x_ref, g_ref, w_in_ref, qa_ref, wq_ref, kva_ref, wk_ref, wv_ref,
                  qh_col_ref, kh_ref, kh_col_ref, wp_ref, ps_ref, cos_ref, sin_ref,
                  pool_ref, qt_ref, k_ref, vt_ref, tail_ref):
    i = pl.program_id(0)
    tm = x_ref.shape[0]

    @pl.when(i == 0)
    def _():
        tail_ref[...] = jnp.zeros_like(tail_ref)

    hn = _rms(x_ref[...], g_ref[...]).astype(BF16)
    z = jnp.dot(hn, w_in_ref[...], preferred_element_type=F32)
    c1 = POOL_DIM
    c2 = c1 + Q_LORA_RANK
    c3 = c2 + KV_LORA_RANK
    u = z[:, :c1]
    q_lat = z[:, c1:c2]
    kv_lat = z[:, c2:c3]
    k_rope = z[:, c3:]

    ue = jnp.concatenate([tail_ref[...], u], axis=0)
    tail_ref[...] = u[tm - POOL_HALO:, :]
    pos = i * tm + lax.broadcasted_iota(jnp.int32, (tm, 1), 0)
    pooled_out = []
    for g, w in enumerate(POOL_WINDOWS):
        a = ue[:, g * POOL_GROUP_DIM:(g + 1) * POOL_GROUP_DIM]
        shift = 1
        while shift < w:
            a = a + pltpu.roll(a, shift, axis=0)
            shift *= 2
        wsum = a[POOL_HALO:, :]
        cnt = jnp.minimum(pos + 1, w).astype(F32)
        pooled = wsum / cnt - u[:, g * POOL_GROUP_DIM:(g + 1) * POOL_GROUP_DIM]
        pooled_out.append(jnp.dot(pooled.astype(BF16), wp_ref[g], preferred_element_type=F32))
    pool_ref[...] = (jnp.concatenate(pooled_out, axis=-1) * ps_ref[...]).astype(BF16)

    cos_t = cos_ref[...]
    sin_t = sin_ref[...]

    qn = _rms(q_lat, qa_ref[...]).astype(BF16)
    q = jnp.dot(qn, wq_ref[...], preferred_element_type=F32)
    scale = QK_HEAD_DIM ** -0.5 * LOG2_E
    for h in range(MLA_HEADS):
        qt = q[:, h * HEAD_PAD:(h + 1) * HEAD_PAD].T
        ms = jnp.sum(qt * qt, axis=0, keepdims=True) * (1.0 / QK_HEAD_DIM)
        r = lax.rsqrt(ms + NORM_EPS) * scale
        qt_ref[h] = (_gain_rope_t(qt, qh_col_ref[...], cos_t, sin_t) * r).astype(BF16)

    kvn = _rms(kv_lat, kva_ref[...]).astype(BF16)
    kk = jnp.dot(kvn, wk_ref[...], preferred_element_type=F32)
    vv = jnp.dot(kvn, wv_ref[...], preferred_element_type=F32)
    ones_lane = (lax.broadcasted_iota(jnp.int32, (1, HEAD_PAD), 1) == V_HEAD_DIM).astype(F32)
    gk = kh_ref[...]
    rope_rot = _gain_rope_t(k_rope.T, kh_col_ref[...], cos_t, sin_t).T
    rope_ssq = jnp.sum(k_rope * k_rope, axis=-1, keepdims=True)
    for h in range(MLA_HEADS):
        kn = kk[:, h * HEAD_PAD:(h + 1) * HEAD_PAD]
        ms = (jnp.sum(kn * kn, axis=-1, keepdims=True) + rope_ssq) * (1.0 / QK_HEAD_DIM)
        k_ref[h] = ((kn * gk + rope_rot) * lax.rsqrt(ms + NORM_EPS)).astype(BF16)
        vh = vv[:, h * HEAD_PAD:(h + 1) * HEAD_PAD] + ones_lane
        vt_ref[h] = vh.T[:V_ROWS].astype(BF16)


def _mla_pre(x, g, w_in, qa, wq, kva, wk, wv, qh, kh, wp, ps, cos_t, sin_t):
    s, d = x.shape
    tm = MLA_PRE_TILE
    row = lambda i: (i, 0)
    half = QK_ROPE_DIM // 2
    return pl.pallas_call(
        _mla_pre_body,
        grid=(s // tm,),
        in_specs=[
            pl.BlockSpec((tm, d), row),
            _const_spec((1, d)),
            _const_spec(w_in.shape),
            _const_spec((1, Q_LORA_RANK)),
            _const_spec(wq.shape),
            _const_spec((1, KV_LORA_RANK)),
            _const_spec(wk.shape),
            _const_spec(wv.shape),
            _const_spec((HEAD_PAD, 1)),
            _const_spec((1, HEAD_PAD)),
            _const_spec((HEAD_PAD, 1)),
            _const_spec(wp.shape),
            _const_spec((1, POOL_DIM)),
            pl.BlockSpec((half, tm), lambda i: (0, i)),
            pl.BlockSpec((half, tm), lambda i: (0, i)),
        ],
        out_specs=[
            pl.BlockSpec((tm, POOL_DIM), row),
            pl.BlockSpec((MLA_HEADS, HEAD_PAD, tm), lambda i: (0, 0, i)),
            pl.BlockSpec((MLA_HEADS, tm, HEAD_PAD), lambda i: (0, i, 0)),
            pl.BlockSpec((MLA_HEADS, V_ROWS, tm), lambda i: (0, 0, i)),
        ],
        out_shape=[
            jax.ShapeDtypeStruct((s, POOL_DIM), BF16),
            jax.ShapeDtypeStruct((MLA_HEADS, HEAD_PAD, s), BF16),
            jax.ShapeDtypeStruct((MLA_HEADS, s, HEAD_PAD), BF16),
            jax.ShapeDtypeStruct((MLA_HEADS, V_ROWS, s), BF16),
        ],
        scratch_shapes=[pltpu.VMEM((POOL_HALO, POOL_DIM), F32)],
        compiler_params=_params(("arbitrary",)),
        name="mla_pre",
    )(x, g.reshape(1, d), w_in, qa.reshape(1, -1), wq, kva.reshape(1, -1), wk, wv,
      qh.reshape(HEAD_PAD, 1), kh, kh.reshape(HEAD_PAD, 1), wp, ps.reshape(1, -1), cos_t, sin_t)


def _flash_body(qt_ref, qt_next_ref, k_ref, vt_ref, o_ref, s_refs, bm_refs, m_ref, acc_ref):
    i = pl.program_id(1)
    tq = qt_ref.shape[1]
    tk = ATTN_K_TILE
    m_ref[...] = jnp.full_like(m_ref, MASK_VALUE)
    acc_ref[...] = jnp.zeros_like(acc_ref)

    def scores(j, buf, c0=0, q_ref=qt_ref):
        off = pl.multiple_of(j * tk, tk)
        st = jnp.dot(k_ref[pl.ds(off, tk), :], q_ref[:, c0:], preferred_element_type=F32)
        s_refs[buf][:, c0:] = st
        bm_refs[buf][:, c0:] = jnp.max(st, axis=0, keepdims=True)

    def consume(j, buf, c0=0, diagonal=False):
        off = pl.multiple_of(j * tk, tk)
        st = s_refs[buf][:, c0:]
        if diagonal:
            row = lax.broadcasted_iota(jnp.int32, (tk, tk), 0)
            col = lax.broadcasted_iota(jnp.int32, (tk, tk), 1)
            square = jnp.where(row <= col, st[:, :tk], MASK_VALUE)
            st = square if tq - c0 == tk else jnp.concatenate([square, st[:, tk:]], axis=1)
            block_max = jnp.max(st, axis=0, keepdims=True)
        else:
            block_max = bm_refs[buf][:, c0:]
        m_old = m_ref[:, c0:]
        m_new = jnp.maximum(m_old, block_max)
        p = jnp.exp2(st - m_new).astype(BF16)
        alpha = jnp.exp2(m_old - m_new)
        vt = vt_ref[:, pl.ds(off, tk)]
        acc_ref[:, c0:] = acc_ref[:, c0:] * alpha + jnp.dot(vt, p, preferred_element_type=F32)
        m_ref[:, c0:] = m_new

    nb = ATTN_BLOCKS_PER_Q
    pairs = ((0, 1), (2, 3))

    @pl.when(i == 0)
    def _():
        scores(0, 0)
        scores(1, 1)

    def run_blocks(j, count):
        for ph in range(count // 2):
            drain, fill = pairs[ph % 2], pairs[(ph + 1) % 2]
            b = j + 2 * ph
            scores(b + 2, fill[0])
            consume(b, drain[0])
            scores(b + 3, fill[1])
            consume(b + 1, drain[1])

    full = nb * i
    trips = full // ATTN_BLOCKS_PER_TRIP

    def body(jj, carry):
        run_blocks(ATTN_BLOCKS_PER_TRIP * jj, ATTN_BLOCKS_PER_TRIP)
        return carry

    lax.fori_loop(0, trips, body, 0)

    def diagonal_blocks(j):
        for ph in range(nb // 2):
            drain, fill = pairs[ph % 2], pairs[(ph + 1) % 2]
            d = 2 * ph
            last = d + 2 >= nb
            frees_first_pair = d + 4 >= nb and not last
            if not last:
                scores(j + d + 2, fill[0], c0=(d + 2) * tk)
            consume(j + d, drain[0], c0=d * tk, diagonal=True)
            if frees_first_pair:
                scores(0, drain[0], q_ref=qt_next_ref)
            if not last:
                scores(j + d + 3, fill[1], c0=(d + 3) * tk)
            consume(j + d + 1, drain[1], c0=(d + 1) * tk, diagonal=True)
            if frees_first_pair:
                scores(1, drain[1], q_ref=qt_next_ref)
        acc = acc_ref[...]
        o_ref[...] = (acc[:V_HEAD_DIM] / acc[V_HEAD_DIM:V_HEAD_DIM + 1]).astype(BF16)

    for rest in range(0, ATTN_BLOCKS_PER_TRIP, nb):
        @pl.when(full - trips * ATTN_BLOCKS_PER_TRIP == rest)
        def _():
            run_blocks(trips * ATTN_BLOCKS_PER_TRIP, rest)
            diagonal_blocks(full)


def _flash(qt, k, vt):
    h, s, dp = k.shape
    tq, tk = ATTN_Q_TILE, ATTN_K_TILE
    assert tq == ATTN_BLOCKS_PER_Q * tk and ATTN_BLOCKS_PER_Q % 4 == 0
    n_buf = 4
    n_q = s // tq
    return pl.pallas_call(
        _flash_body,
        grid=(h, n_q),
        in_specs=[
            pl.BlockSpec((None, dp, tq), lambda hh, i: (hh, 0, i)),
            pl.BlockSpec((None, dp, tq), lambda hh, i: (hh, 0, jnp.minimum(i + 1, n_q - 1))),
            pl.BlockSpec((None, s, dp), lambda hh, i: (hh, 0, 0)),
            pl.BlockSpec((None, V_ROWS, s), lambda hh, i: (hh, 0, 0)),
        ],
        out_specs=pl.BlockSpec((V_HEAD_DIM, tq), lambda hh, i: (hh, i)),
        out_shape=jax.ShapeDtypeStruct((h * V_HEAD_DIM, s), BF16),
        scratch_shapes=[[pltpu.VMEM((tk, tq), F32)] * n_buf, [pltpu.VMEM((1, tq), F32)] * n_buf,
                        pltpu.VMEM((1, tq), F32), pltpu.VMEM((V_ROWS, tq), F32)],
        compiler_params=_params(("arbitrary", "arbitrary")),
        name="flash",
    )(qt, qt, k, vt)


def _conv_body(x_ref, g_ref, w_in_ref, cw_ref, w_out_ref, o_ref, tail_ref):
    i = pl.program_id(0)
    tm, d = x_ref.shape

    @pl.when(i == 0)
    def _():
        tail_ref[...] = jnp.zeros_like(tail_ref)

    x = x_ref[...]
    hn = _rms(x, g_ref[...]).astype(w_in_ref.dtype)
    z = jnp.dot(hn, w_in_ref[...], preferred_element_type=F32)
    gb = z[:, :d]
    u = z[:, d:2 * d] * z[:, 2 * d:]
    ue = jnp.concatenate([tail_ref[...], u], axis=0)
    tail_ref[...] = u[tm - CONV_HALO:, :]
    cw = cw_ref[...]
    y = cw[CONV_WIDTH - 1:CONV_WIDTH] * u
    for back in range(1, CONV_WIDTH):
        tap = CONV_WIDTH - 1 - back
        y = y + cw[tap:tap + 1] * pltpu.roll(ue, back, axis=0)[CONV_HALO:, :]
    y = (gb * y).astype(w_out_ref.dtype)
    o_ref[...] = x + jnp.dot(y, w_out_ref[...], preferred_element_type=F32)


def _conv_mixer(x, layer, g, idx, w_in, cw, w_out):
    s, d = x.shape
    tm = CONV_TILE
    row = lambda i: (i, 0)
    g = _rows(g)
    return pl.pallas_call(
        _conv_body,
        grid=(s // tm,),
        in_specs=[
            pl.BlockSpec((tm, d), row),
            _layer_spec(g, layer),
            _const_spec(w_in.shape),
            _layer_spec(cw, idx),
            _const_spec(w_out.shape),
        ],
        out_specs=pl.BlockSpec((tm, d), row),
        out_shape=jax.ShapeDtypeStruct((s, d), F32),
        scratch_shapes=[pltpu.VMEM((CONV_HALO, d), F32)],
        compiler_params=_params(("arbitrary",)),
        name="conv_mixer",
    )(x, g, w_in, cw, w_out)


def _pad_heads(w, head_dim, take_from, take_n):
    rows = w.shape[0]
    w = w.reshape(rows, MLA_HEADS, head_dim)[:, :, take_from:take_from + take_n]
    w = jnp.pad(w, ((0, 0), (0, 0), (0, HEAD_PAD - take_n)))
    return w.reshape(rows, MLA_HEADS * HEAD_PAD)


def _rope_tables(s):
    pos = jnp.arange(s, dtype=F32)
    inv_freq = ROPE_THETA ** (-jnp.arange(0, QK_ROPE_DIM, 2, dtype=F32) / QK_ROPE_DIM)
    ang = inv_freq[:, None] * pos[None, :]
    return jnp.cos(ang), jnp.sin(ang)


def kernel(x, ffn1_norm, ffn1_w_gate, ffn1_w_up, ffn1_w_down, mix_norm, ffn2_norm, ffn2_w_gate, ffn2_w_up, ffn2_w_down, a_w_in, a_q_a_norm, a_w_q_up, a_kv_a_norm, a_w_kv_up, a_q_head_norm, a_k_head_norm, a_w_pool, a_pool_scale, a_w_out, c_w_in, c_conv_w, c_w_out):
    b, s, d = x.shape
    assert b == 1 and d == D_MODEL and s % TOKEN_TILE == 0 and s % ATTN_Q_TILE == 0
    h = x.reshape(s, d)
    cos_t, sin_t = _rope_tables(s)
    head_pad = ((0, 0), (0, HEAD_PAD - QK_HEAD_DIM))

    ffn_stacks = []
    for layer in range(DEPTH):
        ffn_stacks.append((layer, ffn1_norm, ffn1_w_gate, ffn1_w_up, ffn1_w_down))
        ffn_stacks.append((layer, ffn2_norm, ffn2_w_gate, ffn2_w_up, ffn2_w_down))
    ffn_weights = tuple((w, 0) for w in ffn_stacks[0][2:])

    def ffn_step(h, n, ffn_weights, mix=None, also_cast=()):
        layer, norm = ffn_stacks[n][:2]
        cast_next = ()
        if n + 1 < len(ffn_stacks):
            nxt = ffn_stacks[n + 1]
            cast_next = tuple((w, nxt[0]) for w in nxt[2:])
        h, cast = _ffn(h, norm[layer].reshape(1, d), ffn_weights, mix=mix,
                       cast_next=cast_next + tuple(also_cast))
        return h, cast[:len(cast_next)], cast[len(cast_next):]

    for layer in range(DEPTH):
        i = layer // 2
        conv_f32 = ((c_w_in, i), (c_w_out, i)) if layer % 2 == 1 else ()
        h, ffn_weights, conv_weights = ffn_step(h, 2 * layer, ffn_weights, also_cast=conv_f32)
        if layer % 2 == 0:
            w_in = a_w_in[i]
            c3 = POOL_DIM + Q_LORA_RANK + KV_LORA_RANK
            w_in_p = jnp.concatenate(
                [w_in[:, :c3], jnp.zeros((d, QK_NOPE_DIM), F32), w_in[:, c3:],
                 jnp.zeros((d, HEAD_PAD - QK_HEAD_DIM), F32)], axis=-1).astype(BF16)
            wq = _pad_heads(a_w_q_up[i], QK_HEAD_DIM, 0, QK_HEAD_DIM).astype(BF16)
            wk = _pad_heads(a_w_kv_up[i], QK_NOPE_DIM + V_HEAD_DIM, 0, QK_NOPE_DIM).astype(BF16)
            wv = _pad_heads(a_w_kv_up[i], QK_NOPE_DIM + V_HEAD_DIM, QK_NOPE_DIM, V_HEAD_DIM).astype(BF16)
            qh = jnp.pad(a_q_head_norm[i].reshape(1, -1), head_pad)
            kh = jnp.pad(a_k_head_norm[i].reshape(1, -1), head_pad)
            pool, qt, k, vt = _mla_pre(h, mix_norm[layer], w_in_p, a_q_a_norm[i], wq, a_kv_a_norm[i],
                                      wk, wv, qh, kh, a_w_pool[i].astype(BF16), a_pool_scale[i],
                                      cos_t, sin_t)
            attn_t = _flash(qt, k, vt)
            mix = (pool, attn_t, i, a_w_out)
        else:
            h = _conv_mixer(h, layer, mix_norm, i, conv_weights[0], c_conv_w, conv_weights[1])
            mix = None
        h, ffn_weights, _ = ffn_step(h, 2 * layer + 1, ffn_weights, mix=mix)
    return h.reshape(b, s, d)
```

```python
import functools

import jax
import jax.numpy as jnp
from jax import lax
from jax.experimental import pallas as pl
from jax.experimental.pallas import tpu as pltpu

D_MODEL = 1024
DEPTH = 4
D_FF = 2816
POOL_WINDOWS = (2, 4, 8, 16)
POOL_DIM = 512
POOL_GROUP_DIM = 128
MLA_HEADS = 8
QK_NOPE_DIM = 64
QK_ROPE_DIM = 32
QK_HEAD_DIM = 96
V_HEAD_DIM = 64
Q_LORA_RANK = 384
KV_LORA_RANK = 256
ROPE_THETA = 10000.0
CONV_WIDTH = 3
NORM_EPS = 1e-6

LANES = 128
HEAD_PAD = LANES
MIX_IN_PAD = POOL_DIM + Q_LORA_RANK + KV_LORA_RANK + HEAD_PAD
POOL_HALO = 16
CONV_HALO = 8
MASK_VALUE = -1e30
VMEM_LIMIT = 56 * 1024 * 1024

V_ROWS = 80
LOG2_E = 1.4426950408889634

TOKEN_TILE = 512
MLA_PRE_TILE = 1024
CONV_TILE = 1024
FFN_TILE = 1024
FFN_CAST_SLABS = 16
FFN_CHUNKS = ((0, 1024), (1024, 2048), (2048, D_FF))
ATTN_Q_TILE = 1024
ATTN_K_TILE = 256
ATTN_BLOCKS_PER_Q = ATTN_Q_TILE // ATTN_K_TILE
ATTN_BLOCKS_PER_TRIP = 8 * ATTN_BLOCKS_PER_Q

F32 = jnp.float32
BF16 = jnp.bfloat16


def _rms(x, g, n=None):
    n = x.shape[-1] if n is None else n
    ms = jnp.sum(x * x, axis=-1, keepdims=True) * (1.0 / n)
    return x * lax.rsqrt(ms + NORM_EPS) * g


def _const_spec(shape):
    nd = len(shape)
    return pl.BlockSpec(shape, lambda *_: (0,) * nd, pipeline_mode=pl.Buffered(1))


def _layer_spec(stacked, layer):
    nd = stacked.ndim
    return pl.BlockSpec((None,) + stacked.shape[1:], lambda *_: (layer,) + (0,) * (nd - 1),
                        pipeline_mode=pl.Buffered(1))


def _rows(v):
    return v.reshape(v.shape[0], 1, v.shape[1])


def _params(sem):
    return pltpu.CompilerParams(dimension_semantics=sem, vmem_limit_bytes=VMEM_LIMIT)


def _ffn_half_step(x, g_ref, wg_ref, wu_ref, wd_ref):
    xn = _rms(x, g_ref[...]).astype(wg_ref.dtype)
    y = None
    for lo, hi in FFN_CHUNKS:
        gate = jnp.dot(xn, wg_ref[:, lo:hi], preferred_element_type=F32)
        up = jnp.dot(xn, wu_ref[:, lo:hi], preferred_element_type=F32)
        h = (gate * jax.nn.sigmoid(gate) * up).astype(wd_ref.dtype)
        part = jnp.dot(h, wd_ref[lo:hi, :], preferred_element_type=F32)
        y = part if y is None else y + part
    return x + 0.5 * y


def _ffn_body(has_mix, n_cast, *refs):
    refs = list(refs)
    x = refs.pop(0)[...]
    if has_mix:
        pool_ref, at_ref, wo_ref = refs[:3]
        del refs[:3]
        wo_pool = wo_ref[:POOL_DIM, :].astype(BF16)
        wo_attn = wo_ref[POOL_DIM:, :].astype(BF16)
        x = x + jnp.dot(pool_ref[...], wo_pool, preferred_element_type=F32)
        x = x + lax.dot_general(at_ref[...], wo_attn, (((0,), (0,)), ((), ())),
                                preferred_element_type=F32)
    g_ref, wg_ref, wu_ref, wd_ref = refs[:4]
    del refs[:4]
    src = refs[:n_cast]
    o_ref = refs[n_cast]
    o_ref[...] = _ffn_half_step(x, g_ref, wg_ref, wu_ref, wd_ref)
    for s_ref, d_ref in zip(src, refs[n_cast + 1:]):
        d_ref[...] = s_ref[...].astype(BF16)


def _ffn(x, g_row, weights, mix=None, cast_next=()):
    s, d = x.shape
    from_f32 = isinstance(weights[0], tuple)
    tm = FFN_TILE if mix is None and not from_f32 else TOKEN_TILE
    steps = s // tm
    row = lambda i: (i, 0)
    operands = [x]
    in_specs = [pl.BlockSpec((tm, d), row)]
    if mix is not None:
        pool, attn_t, idx, wo = mix
        operands += [pool, attn_t, wo]
        in_specs += [pl.BlockSpec((tm, POOL_DIM), row),
                     pl.BlockSpec((attn_t.shape[0], tm), lambda i: (0, i)), _layer_spec(wo, idx)]
    operands.append(g_row)
    in_specs.append(_const_spec(g_row.shape))
    for w in weights:
        operands.append(w[0] if from_f32 else w)
        in_specs.append(_layer_spec(*w) if from_f32 else _const_spec(w.shape))
    out_specs = [pl.BlockSpec((tm, d), row)]
    out_shape = [jax.ShapeDtypeStruct((s, d), F32)]
    per_slab = steps // FFN_CAST_SLABS
    for w, idx in cast_next:
        rows = w.shape[1] // FFN_CAST_SLABS
        operands.append(w)
        in_specs.append(pl.BlockSpec((None, rows, w.shape[2]),
                                     lambda i, idx=idx: (idx, i // per_slab, 0)))
        out_specs.append(pl.BlockSpec((rows, w.shape[2]), lambda i: (i // per_slab, 0)))
        out_shape.append(jax.ShapeDtypeStruct(w.shape[1:], BF16))
    outs = pl.pallas_call(
        functools.partial(_ffn_body, mix is not None, len(cast_next)),
        grid=(steps,),
        in_specs=in_specs,
        out_specs=out_specs,
        out_shape=out_shape,
        compiler_params=_params(("arbitrary",)),
        name="ffn" if mix is None else "mix_ffn",
    )(*operands)
    return outs[0], tuple(outs[1:])


def _gain_rope_t(xt, gain_col, cos_t, sin_t):
    half = QK_ROPE_DIM // 2
    a, b, c = QK_NOPE_DIM, QK_NOPE_DIM + half, QK_HEAD_DIM
    y = xt * gain_col
    x1, x2 = y[a:b], y[b:c]
    return jnp.concatenate([y[:a], x1 * cos_t - x2 * sin_t, x2 * cos_t + x1 * sin_t, y[c:]], axis=0)


def _mla_pre_body(x_ref, g_ref, w_in_ref, qa_ref, wq_ref, kva_ref, wk_ref, wv_ref,
                  qh_col_ref, kh_ref, kh_col_ref, wp_ref, ps_ref, cos_ref, sin_ref,
                  pool_ref, qt_ref, k_ref, vt_ref, tail_ref):
    i = pl.program_id(0)
    tm = x_ref.shape[0]

    @pl.when(i == 0)
    def _():
        tail_ref[...] = jnp.zeros_like(tail_ref)

    hn = _rms(x_ref[...], g_ref[...]).astype(BF16)
    z = jnp.dot(hn, w_in_ref[...], preferred_element_type=F32)
    c1 = POOL_DIM
    c2 = c1 + Q_LORA_RANK
    c3 = c2 + KV_LORA_RANK
    u = z[:, :c1]
    q_lat = z[:, c1:c2]
    kv_lat = z[:, c2:c3]
    k_rope = z[:, c3:]

    ue = jnp.concatenate([tail_ref[...], u], axis=0)
    tail_ref[...] = u[tm - POOL_HALO:, :]
    pos = i * tm + lax.broadcasted_iota(jnp.int32, (tm, 1), 0)
    pooled_out = []
    for g, w in enumerate(POOL_WINDOWS):
        a = ue[:, g * POOL_GROUP_DIM:(g + 1) * POOL_GROUP_DIM]
        shift = 1
        while shift < w:
            a = a + pltpu.roll(a, shift, axis=0)
            shift *= 2
        wsum = a[POOL_HALO:, :]
        cnt = jnp.minimum(pos + 1, w).astype(F32)
        pooled = wsum / cnt - u[:, g * POOL_GROUP_DIM:(g + 1) * POOL_GROUP_DIM]
        pooled_out.append(jnp.dot(pooled.astype(BF16), wp_ref[g], preferred_element_type=F32))
    pool_ref[...] = (jnp.concatenate(pooled_out, axis=-1) * ps_ref[...]).astype(BF16)

    cos_t = cos_ref[...]
    sin_t = sin_ref[...]

    qn = _rms(q_lat, qa_ref[...]).astype(BF16)
    q = jnp.dot(qn, wq_ref[...], preferred_element_type=F32)
    scale = QK_HEAD_DIM ** -0.5 * LOG2_E
    for h in range(MLA_HEADS):
        qt = q[:, h * HEAD_PAD:(h + 1) * HEAD_PAD].T
        ms = jnp.sum(qt * qt, axis=0, keepdims=True) * (1.0 / QK_HEAD_DIM)
        r = lax.rsqrt(ms + NORM_EPS) * scale
        qt_ref[h] = (_gain_rope_t(qt, qh_col_ref[...], cos_t, sin_t) * r).astype(BF16)

    kvn = _rms(kv_lat, kva_ref[...]).astype(BF16)
    kk = jnp.dot(kvn, wk_ref[...], preferred_element_type=F32)
    vv = jnp.dot(kvn, wv_ref[...], preferred_element_type=F32)
    ones_lane = (lax.broadcasted_iota(jnp.int32, (1, HEAD_PAD), 1) == V_HEAD_DIM).astype(F32)
    gk = kh_ref[...]
    rope_rot = _gain_rope_t(k_rope.T, kh_col_ref[...], cos_t, sin_t).T
    rope_ssq = jnp.sum(k_rope * k_rope, axis=-1, keepdims=True)
    for h in range(MLA_HEADS):
        kn = kk[:, h * HEAD_PAD:(h + 1) * HEAD_PAD]
        ms = (jnp.sum(kn * kn, axis=-1, keepdims=True) + rope_ssq) * (1.0 / QK_HEAD_DIM)
        k_ref[h] = ((kn * gk + rope_rot) * lax.rsqrt(ms + NORM_EPS)).astype(BF16)
        vh = vv[:, h * HEAD_PAD:(h + 1) * HEAD_PAD] + ones_lane
        vt_ref[h] = vh.T[:V_ROWS].astype(BF16)


def _mla_pre(x, g, w_in, qa, wq, kva, wk, wv, qh, kh, wp, ps, cos_t, sin_t):
    s, d = x.shape
    tm = MLA_PRE_TILE
    row = lambda i: (i, 0)
    half = QK_ROPE_DIM // 2
    return pl.pallas_call(
        _mla_pre_body,
        grid=(s // tm,),
        in_specs=[
            pl.BlockSpec((tm, d), row),
            _const_spec((1, d)),
            _const_spec(w_in.shape),
            _const_spec((1, Q_LORA_RANK)),
            _const_spec(wq.shape),
            _const_spec((1, KV_LORA_RANK)),
            _const_spec(wk.shape),
            _const_spec(wv.shape),
            _const_spec((HEAD_PAD, 1)),
            _const_spec((1, HEAD_PAD)),
            _const_spec((HEAD_PAD, 1)),
            _const_spec(wp.shape),
            _const_spec((1, POOL_DIM)),
            pl.BlockSpec((half, tm), lambda i: (0, i)),
            pl.BlockSpec((half, tm), lambda i: (0, i)),
        ],
        out_specs=[
            pl.BlockSpec((tm, POOL_DIM), row),
            pl.BlockSpec((MLA_HEADS, HEAD_PAD, tm), lambda i: (0, 0, i)),
            pl.BlockSpec((MLA_HEADS, tm, HEAD_PAD), lambda i: (0, i, 0)),
            pl.BlockSpec((MLA_HEADS, V_ROWS, tm), lambda i: (0, 0, i)),
        ],
        out_shape=[
            jax.ShapeDtypeStruct((s, POOL_DIM), BF16),
            jax.ShapeDtypeStruct((MLA_HEADS, HEAD_PAD, s), BF16),
            jax.ShapeDtypeStruct((MLA_HEADS, s, HEAD_PAD), BF16),
            jax.ShapeDtypeStruct((MLA_HEADS, V_ROWS, s), BF16),
        ],
        scratch_shapes=[pltpu.VMEM((POOL_HALO, POOL_DIM), F32)],
        compiler_params=_params(("arbitrary",)),
        name="mla_pre",
    )(x, g.reshape(1, d), w_in, qa.reshape(1, -1), wq, kva.reshape(1, -1), wk, wv,
      qh.reshape(HEAD_PAD, 1), kh, kh.reshape(HEAD_PAD, 1), wp, ps.reshape(1, -1), cos_t, sin_t)


def _flash_body(qt_ref, qt_next_ref, k_ref, vt_ref, o_ref, s_refs, bm_refs, m_ref, acc_ref):
    i = pl.program_id(1)
    tq = qt_ref.shape[1]
    tk = ATTN_K_TILE
    m_ref[...] = jnp.full_like(m_ref, MASK_VALUE)
    acc_ref[...] = jnp.zeros_like(acc_ref)

    def scores(j, buf, c0=0, q_ref=qt_ref):
        off = pl.multiple_of(j * tk, tk)
        st = jnp.dot(k_ref[pl.ds(off, tk), :], q_ref[:, c0:], preferred_element_type=F32)
        s_refs[buf][:, c0:] = st
        bm_refs[buf][:, c0:] = jnp.max(st, axis=0, keepdims=True)

    def consume(j, buf, c0=0, diagonal=False):
        off = pl.multiple_of(j * tk, tk)
        st = s_refs[buf][:, c0:]
        if diagonal:
            row = lax.broadcasted_iota(jnp.int32, (tk, tk), 0)
            col = lax.broadcasted_iota(jnp.int32, (tk, tk), 1)
            square = jnp.where(row <= col, st[:, :tk], MASK_VALUE)
            st = square if tq - c0 == tk else jnp.concatenate([square, st[:, tk:]], axis=1)
            block_max = jnp.max(st, axis=0, keepdims=True)
        else:
            block_max = bm_refs[buf][:, c0:]
        m_old = m_ref[:, c0:]
        m_new = jnp.maximum(m_old, block_max)
        p = jnp.exp2(st - m_new).astype(BF16)
        alpha = jnp.exp2(m_old - m_new)
        vt = vt_ref[:, pl.ds(off, tk)]
        acc_ref[:, c0:] = acc_ref[:, c0:] * alpha + jnp.dot(vt, p, preferred_element_type=F32)
        m_ref[:, c0:] = m_new

    nb = ATTN_BLOCKS_PER_Q
    pairs = ((0, 1), (2, 3))

    @pl.when(i == 0)
    def _():
        scores(0, 0)
        scores(1, 1)

    def run_blocks(j, count):
        for ph in range(count // 2):
            drain, fill = pairs[ph % 2], pairs[(ph + 1) % 2]
            b = j + 2 * ph
            scores(b + 2, fill[0])
            consume(b, drain[0])
            scores(b + 3, fill[1])
            consume(b + 1, drain[1])

    full = nb * i
    trips = full // ATTN_BLOCKS_PER_TRIP

    def body(jj, carry):
        run_blocks(ATTN_BLOCKS_PER_TRIP * jj, ATTN_BLOCKS_PER_TRIP)
        return carry

    lax.fori_loop(0, trips, body, 0)

    def diagonal_blocks(j):
        for ph in range(nb // 2):
            drain, fill = pairs[ph % 2], pairs[(ph + 1) % 2]
            d = 2 * ph
            last = d + 2 >= nb
            frees_first_pair = d + 4 >= nb and not last
            if not last:
                scores(j + d + 2, fill[0], c0=(d + 2) * tk)
            consume(j + d, drain[0], c0=d * tk, diagonal=True)
            if frees_first_pair:
                scores(0, drain[0], q_ref=qt_next_ref)
            if not last:
                scores(j + d + 3, fill[1], c0=(d + 3) * tk)
            consume(j + d + 1, drain[1], c0=(d + 1) * tk, diagonal=True)
            if frees_first_pair:
                scores(1, drain[1], q_ref=qt_next_ref)
        acc = acc_ref[...]
        o_ref[...] = (acc[:V_HEAD_DIM] / acc[V_HEAD_DIM:V_HEAD_DIM + 1]).astype(BF16)

    for rest in range(0, ATTN_BLOCKS_PER_TRIP, nb):
        @pl.when(full - trips * ATTN_BLOCKS_PER_TRIP == rest)
        def _():
            run_blocks(trips * ATTN_BLOCKS_PER_TRIP, rest)
            diagonal_blocks(full)


def _flash(qt, k, vt):
    h, s, dp = k.shape
    tq, tk = ATTN_Q_TILE, ATTN_K_TILE
    assert tq == ATTN_BLOCKS_PER_Q * tk and ATTN_BLOCKS_PER_Q % 4 == 0
    n_buf = 4
    n_q = s // tq
    return pl.pallas_call(
        _flash_body,
        grid=(h, n_q),
        in_specs=[
            pl.BlockSpec((None, dp, tq), lambda hh, i: (hh, 0, i)),
            pl.BlockSpec((None, dp, tq), lambda hh, i: (hh, 0, jnp.minimum(i + 1, n_q - 1))),
            pl.BlockSpec((None, s, dp), lambda hh, i: (hh, 0, 0)),
            pl.BlockSpec((None, V_ROWS, s), lambda hh, i: (hh, 0, 0)),
        ],
        out_specs=pl.BlockSpec((V_HEAD_DIM, tq), lambda hh, i: (hh, i)),
        out_shape=jax.ShapeDtypeStruct((h * V_HEAD_DIM, s), BF16),
        scratch_shapes=[[pltpu.VMEM((tk, tq), F32)] * n_buf, [pltpu.VMEM((1, tq), F32)] * n_buf,
                        pltpu.VMEM((1, tq), F32), pltpu.VMEM((V_ROWS, tq), F32)],
        compiler_params=_params(("arbitrary", "arbitrary")),
        name="flash",
    )(qt, qt, k, vt)


def _conv_body(x_ref, g_ref, w_in_ref, cw_ref, w_out_ref, o_ref, tail_ref):
    i = pl.program_id(0)
    tm, d = x_ref.shape

    @pl.when(i == 0)
    def _():
        tail_ref[...] = jnp.zeros_like(tail_ref)

    x = x_ref[...]
    hn = _rms(x, g_ref[...]).astype(w_in_ref.dtype)
    z = jnp.dot(hn, w_in_ref[...], preferred_element_type=F32)
    gb = z[:, :d]
    u = z[:, d:2 * d] * z[:, 2 * d:]
    ue = jnp.concatenate([tail_ref[...], u], axis=0)
    tail_ref[...] = u[tm - CONV_HALO:, :]
    cw = cw_ref[...]
    y = cw[CONV_WIDTH - 1:CONV_WIDTH] * u
    for back in range(1, CONV_WIDTH):
        tap = CONV_WIDTH - 1 - back
        y = y + cw[tap:tap + 1] * pltpu.roll(ue, back, axis=0)[CONV_HALO:, :]
    y = (gb * y).astype(w_out_ref.dtype)
    o_ref[...] = x + jnp.dot(y, w_out_ref[...], preferred_element_type=F32)


def _conv_mixer(x, layer, g, idx, w_in, cw, w_out):
    s, d = x.shape
    tm = CONV_TILE
    row = lambda i: (i, 0)
    g = _rows(g)
    return pl.pallas_call(
        _conv_body,
        grid=(s // tm,),
        in_specs=[
            pl.BlockSpec((tm, d), row),
            _layer_spec(g, layer),
            _const_spec(w_in.shape),
            _layer_spec(cw, idx),
            _const_spec(w_out.shape),
        ],
        out_specs=pl.BlockSpec((tm, d), row),
        out_shape=jax.ShapeDtypeStruct((s, d), F32),
        scratch_shapes=[pltpu.VMEM((CONV_HALO, d), F32)],
        compiler_params=_params(("arbitrary",)),
        name="conv_mixer",
    )(x, g, w_in, cw, w_out)


def _pad_heads(w, head_dim, take_from, take_n):
    rows = w.shape[0]
    w = w.reshape(rows, MLA_HEADS, head_dim)[:, :, take_from:take_from + take_n]
    w = jnp.pad(w, ((0, 0), (0, 0), (0, HEAD_PAD - take_n)))
    return w.reshape(rows, MLA_HEADS * HEAD_PAD)


def _rope_tables(s):
    pos = jnp.arange(s, dtype=F32)
    inv_freq = ROPE_THETA ** (-jnp.arange(0, QK_ROPE_DIM, 2, dtype=F32) / QK_ROPE_DIM)
    ang = inv_freq[:, None] * pos[None, :]
    return jnp.cos(ang), jnp.sin(ang)


def kernel(x, ffn1_norm, ffn1_w_gate, ffn1_w_up, ffn1_w_down, mix_norm, ffn2_norm, ffn2_w_gate, ffn2_w_up, ffn2_w_down, a_w_in, a_q_a_norm, a_w_q_up, a_kv_a_norm, a_w_kv_up, a_q_head_norm, a_k_head_norm, a_w_pool, a_pool_scale, a_w_out, c_w_in, c_conv_w, c_w_out):
    b, s, d = x.shape
    assert b == 1 and d == D_MODEL and s % TOKEN_TILE == 0 and s % ATTN_Q_TILE == 0
    h = x.reshape(s, d)
    cos_t, sin_t = _rope_tables(s)
    head_pad = ((0, 0), (0, HEAD_PAD - QK_HEAD_DIM))

    ffn_stacks = []
    for layer in range(DEPTH):
        ffn_stacks.append((layer, ffn1_norm, ffn1_w_gate, ffn1_w_up, ffn1_w_down))
        ffn_stacks.append((layer, ffn2_norm, ffn2_w_gate, ffn2_w_up, ffn2_w_down))
    ffn_weights = tuple((w, 0) for w in ffn_stacks[0][2:])

    def ffn_step(h, n, ffn_weights, mix=None, also_cast=()):
        layer, norm = ffn_stacks[n][:2]
        cast_next = ()
        if n + 1 < len(ffn_stacks):
            nxt = ffn_stacks[n + 1]
            cast_next = tuple((w, nxt[0]) for w in nxt[2:])
        h, cast = _ffn(h, norm[layer].reshape(1, d), ffn_weights, mix=mix,
                       cast_next=cast_next + tuple(also_cast))
        return h, cast[:len(cast_next)], cast[len(cast_next):]

    for layer in range(DEPTH):
        i = layer // 2
        conv_f32 = ((c_w_in, i), (c_w_out, i)) if layer % 2 == 1 else ()
        h, ffn_weights, conv_weights = ffn_step(h, 2 * layer, ffn_weights, also_cast=conv_f32)
        if layer % 2 == 0:
            w_in = a_w_in[i]
            c3 = POOL_DIM + Q_LORA_RANK + KV_LORA_RANK
            w_in_p = jnp.concatenate(
                [w_in[:, :c3], jnp.zeros((d, QK_NOPE_DIM), F32), w_in[:, c3:],
                 jnp.zeros((d, HEAD_PAD - QK_HEAD_DIM), F32)], axis=-1).astype(BF16)
            wq = _pad_heads(a_w_q_up[i], QK_HEAD_DIM, 0, QK_HEAD_DIM).astype(BF16)
            wk = _pad_heads(a_w_kv_up[i], QK_NOPE_DIM + V_HEAD_DIM, 0, QK_NOPE_DIM).astype(BF16)
            wv = _pad_heads(a_w_kv_up[i], QK_NOPE_DIM + V_HEAD_DIM, QK_NOPE_DIM, V_HEAD_DIM).astype(BF16)
            qh = jnp.pad(a_q_head_norm[i].reshape(1, -1), head_pad)
            kh = jnp.pad(a_k_head_norm[i].reshape(1, -1), head_pad)
            pool, qt, k, vt = _mla_pre(h, mix_norm[layer], w_in_p, a_q_a_norm[i], wq, a_kv_a_norm[i],
                                      wk, wv, qh, kh, a_w_pool[i].astype(BF16), a_pool_scale[i],
                                      cos_t, sin_t)
            attn_t = _flash(qt, k, vt)
            mix = (pool, attn_t, i, a_w_out)
        else:
            h = _conv_mixer(h, layer, mix_norm, i, conv_weights[0], c_conv_w, conv_weights[1])
            mix = None
        h, ffn_weights, _ = ffn_step(h, 2 * layer + 1, ffn_weights, mix=mix)
    return h.reshape(b, s, d)
```

```python
import functools

import jax
import jax.numpy as jnp
from jax import lax
from jax.experimental import pallas as pl
from jax.experimental.pallas import tpu as pltpu

D_MODEL = 1024
DEPTH = 4
D_FF = 2816
POOL_WINDOWS = (2, 4, 8, 16)
POOL_DIM = 512
POOL_GROUP_DIM = 128
MLA_HEADS = 8
QK_NOPE_DIM = 64
QK_ROPE_DIM = 32
QK_HEAD_DIM = 96
V_HEAD_DIM = 64
Q_LORA_RANK = 384
KV_LORA_RANK = 256
ROPE_THETA = 10000.0
CONV_WIDTH = 3
NORM_EPS = 1e-6

LANES = 128
HEAD_PAD = LANES
MIX_IN_PAD = POOL_DIM + Q_LORA_RANK + KV_LORA_RANK + HEAD_PAD
POOL_HALO = 16
CONV_HALO = 8
MASK_VALUE = -1e30
VMEM_LIMIT = 56 * 1024 * 1024

V_ROWS = 80
LOG2_E = 1.4426950408889634

TOKEN_TILE = 512
MLA_PRE_TILE = 1024
CONV_TILE = 1024
FFN_TILE = 1024
FFN_CAST_SLABS = 16
FFN_CHUNKS = ((0, 1024), (1024, 2048), (2048, D_FF))
ATTN_Q_TILE = 1024
ATTN_K_TILE = 256
ATTN_BLOCKS_PER_Q = ATTN_Q_TILE // ATTN_K_TILE
ATTN_BLOCKS_PER_TRIP = 6 * ATTN_BLOCKS_PER_Q

F32 = jnp.float32
BF16 = jnp.bfloat16


def _rms(x, g, n=None):
    n = x.shape[-1] if n is None else n
    ms = jnp.sum(x * x, axis=-1, keepdims=True) * (1.0 / n)
    return x * lax.rsqrt(ms + NORM_EPS) * g


def _const_spec(shape):
    nd = len(shape)
    return pl.BlockSpec(shape, lambda *_: (0,) * nd, pipeline_mode=pl.Buffered(1))


def _layer_spec(stacked, layer):
    nd = stacked.ndim
    return pl.BlockSpec((None,) + stacked.shape[1:], lambda *_: (layer,) + (0,) * (nd - 1),
                        pipeline_mode=pl.Buffered(1))


def _rows(v):
    return v.reshape(v.shape[0], 1, v.shape[1])


def _params(sem):
    return pltpu.CompilerParams(dimension_semantics=sem, vmem_limit_bytes=VMEM_LIMIT)


def _ffn_half_step(x, g_ref, wg_ref, wu_ref, wd_ref):
    xn = _rms(x, g_ref[...]).astype(wg_ref.dtype)
    y = None
    for lo, hi in FFN_CHUNKS:
        gate = jnp.dot(xn, wg_ref[:, lo:hi], preferred_element_type=F32)
        up = jnp.dot(xn, wu_ref[:, lo:hi], preferred_element_type=F32)
        h = (gate * jax.nn.sigmoid(gate) * up).astype(wd_ref.dtype)
        part = jnp.dot(h, wd_ref[lo:hi, :], preferred_element_type=F32)
        y = part if y is None else y + part
    return x + 0.5 * y


def _ffn_body(has_mix, n_cast, *refs):
    refs = list(refs)
    x = refs.pop(0)[...]
    if has_mix:
        pool_ref, at_ref, wo_ref = refs[:3]
        del refs[:3]
        wo_pool = wo_ref[:POOL_DIM, :].astype(BF16)
        wo_attn = wo_ref[POOL_DIM:, :].astype(BF16)
        x = x + jnp.dot(pool_ref[...], wo_pool, preferred_element_type=F32)
        x = x + lax.dot_general(at_ref[...], wo_attn, (((0,), (0,)), ((), ())),
                                preferred_element_type=F32)
    g_ref, wg_ref, wu_ref, wd_ref = refs[:4]
    del refs[:4]
    src = refs[:n_cast]
    o_ref = refs[n_cast]
    o_ref[...] = _ffn_half_step(x, g_ref, wg_ref, wu_ref, wd_ref)
    for s_ref, d_ref in zip(src, refs[n_cast + 1:]):
        d_ref[...] = s_ref[...].astype(BF16)


def _ffn(x, g_row, weights, mix=None, cast_next=()):
    s, d = x.shape
    from_f32 = isinstance(weights[0], tuple)
    tm = FFN_TILE if mix is None and not from_f32 else TOKEN_TILE
    steps = s // tm
    row = lambda i: (i, 0)
    operands = [x]
    in_specs = [pl.BlockSpec((tm, d), row)]
    if mix is not None:
        pool, attn_t, idx, wo = mix
        operands += [pool, attn_t, wo]
        in_specs += [pl.BlockSpec((tm, POOL_DIM), row),
                     pl.BlockSpec((attn_t.shape[0], tm), lambda i: (0, i)), _layer_spec(wo, idx)]
    operands.append(g_row)
    in_specs.append(_const_spec(g_row.shape))
    for w in weights:
        operands.append(w[0] if from_f32 else w)
        in_specs.append(_layer_spec(*w) if from_f32 else _const_spec(w.shape))
    out_specs = [pl.BlockSpec((tm, d), row)]
    out_shape = [jax.ShapeDtypeStruct((s, d), F32)]
    per_slab = steps // FFN_CAST_SLABS
    for w, idx in cast_next:
        rows = w.shape[1] // FFN_CAST_SLABS
        operands.append(w)
        in_specs.append(pl.BlockSpec((None, rows, w.shape[2]),
                                     lambda i, idx=idx: (idx, i // per_slab, 0)))
        out_specs.append(pl.BlockSpec((rows, w.shape[2]), lambda i: (i // per_slab, 0)))
        out_shape.append(jax.ShapeDtypeStruct(w.shape[1:], BF16))
    outs = pl.pallas_call(
        functools.partial(_ffn_body, mix is not None, len(cast_next)),
        grid=(steps,),
        in_specs=in_specs,
        out_specs=out_specs,
        out_shape=out_shape,
        compiler_params=_params(("arbitrary",)),
        name="ffn" if mix is None else "mix_ffn",
    )(*operands)
    return outs[0], tuple(outs[1:])


def _gain_rope_t(xt, gain_col, cos_t, sin_t):
    half = QK_ROPE_DIM // 2
    a, b, c = QK_NOPE_DIM, QK_NOPE_DIM + half, QK_HEAD_DIM
    y = xt * gain_col
    x1, x2 = y[a:b], y[b:c]
    return jnp.concatenate([y[:a], x1 * cos_t - x2 * sin_t, x2 * cos_t + x1 * sin_t, y[c:]], axis=0)


def _mla_pre_body(x_ref, g_ref, w_in_ref, qa_ref, wq_ref, kva_ref, wk_ref, wv_ref,
                  qh_col_ref, kh_ref, kh_col_ref, wp_ref, ps_ref, cos_ref, sin_ref,
                  pool_ref, qt_ref, k_ref, vt_ref, tail_ref):
    i = pl.program_id(0)
    tm = x_ref.shape[0]

    @pl.when(i == 0)
    def _():
        tail_ref[...] = jnp.zeros_like(tail_ref)

    hn = _rms(x_ref[...], g_ref[...]).astype(BF16)
    z = jnp.dot(hn, w_in_ref[...], preferred_element_type=F32)
    c1 = POOL_DIM
    c2 = c1 + Q_LORA_RANK
    c3 = c2 + KV_LORA_RANK
    u = z[:, :c1]
    q_lat = z[:, c1:c2]
    kv_lat = z[:, c2:c3]
    k_rope = z[:, c3:]

    ue = jnp.concatenate([tail_ref[...], u], axis=0)
    tail_ref[...] = u[tm - POOL_HALO:, :]
    pos = i * tm + lax.broadcasted_iota(jnp.int32, (tm, 1), 0)
    pooled_out = []
    for g, w in enumerate(POOL_WINDOWS):
        a = ue[:, g * POOL_GROUP_DIM:(g + 1) * POOL_GROUP_DIM]
        shift = 1
        while shift < w:
            a = a + pltpu.roll(a, shift, axis=0)
            shift *= 2
        wsum = a[POOL_HALO:, :]
        cnt = jnp.minimum(pos + 1, w).astype(F32)
        pooled = wsum / cnt - u[:, g * POOL_GROUP_DIM:(g + 1) * POOL_GROUP_DIM]
        pooled_out.append(jnp.dot(pooled.astype(BF16), wp_ref[g], preferred_element_type=F32))
    pool_ref[...] = (jnp.concatenate(pooled_out, axis=-1) * ps_ref[...]).astype(BF16)

    cos_t = cos_ref[...]
    sin_t = sin_ref[...]

    qn = _rms(q_lat, qa_ref[...]).astype(BF16)
    q = jnp.dot(qn, wq_ref[...], preferred_element_type=F32)
    scale = QK_HEAD_DIM ** -0.5 * LOG2_E
    for h in range(MLA_HEADS):
        qt = q[:, h * HEAD_PAD:(h + 1) * HEAD_PAD].T
        ms = jnp.sum(qt * qt, axis=0, keepdims=True) * (1.0 / QK_HEAD_DIM)
        r = lax.rsqrt(ms + NORM_EPS) * scale
        qt_ref[h] = (_gain_rope_t(qt, qh_col_ref[...], cos_t, sin_t) * r).astype(BF16)

    kvn = _rms(kv_lat, kva_ref[...]).astype(BF16)
    kk = jnp.dot(kvn, wk_ref[...], preferred_element_type=F32)
    vv = jnp.dot(kvn, wv_ref[...], preferred_element_type=F32)
    ones_lane = (lax.broadcasted_iota(jnp.int32, (1, HEAD_PAD), 1) == V_HEAD_DIM).astype(F32)
    gk = kh_ref[...]
    rope_rot = _gain_rope_t(k_rope.T, kh_col_ref[...], cos_t, sin_t).T
    rope_ssq = jnp.sum(k_rope * k_rope, axis=-1, keepdims=True)
    for h in range(MLA_HEADS):
        kn = kk[:, h * HEAD_PAD:(h + 1) * HEAD_PAD]
        ms = (jnp.sum(kn * kn, axis=-1, keepdims=True) + rope_ssq) * (1.0 / QK_HEAD_DIM)
        k_ref[h] = ((kn * gk + rope_rot) * lax.rsqrt(ms + NORM_EPS)).astype(BF16)
        vh = vv[:, h * HEAD_PAD:(h + 1) * HEAD_PAD] + ones_lane
        vt_ref[h] = vh.T[:V_ROWS].astype(BF16)


def _mla_pre(x, g, w_in, qa, wq, kva, wk, wv, qh, kh, wp, ps, cos_t, sin_t):
    s, d = x.shape
    tm = MLA_PRE_TILE
    row = lambda i: (i, 0)
    half = QK_ROPE_DIM // 2
    return pl.pallas_call(
        _mla_pre_body,
        grid=(s // tm,),
        in_specs=[
            pl.BlockSpec((tm, d), row),
            _const_spec((1, d)),
            _const_spec(w_in.shape),
            _const_spec((1, Q_LORA_RANK)),
            _const_spec(wq.shape),
            _const_spec((1, KV_LORA_RANK)),
            _const_spec(wk.shape),
            _const_spec(wv.shape),
            _const_spec((HEAD_PAD, 1)),
            _const_spec((1, HEAD_PAD)),
            _const_spec((HEAD_PAD, 1)),
            _const_spec(wp.shape),
            _const_spec((1, POOL_DIM)),
            pl.BlockSpec((half, tm), lambda i: (0, i)),
            pl.BlockSpec((half, tm), lambda i: (0, i)),
        ],
        out_specs=[
            pl.BlockSpec((tm, POOL_DIM), row),
            pl.BlockSpec((MLA_HEADS, HEAD_PAD, tm), lambda i: (0, 0, i)),
            pl.BlockSpec((MLA_HEADS, tm, HEAD_PAD), lambda i: (0, i, 0)),
            pl.BlockSpec((MLA_HEADS, V_ROWS, tm), lambda i: (0, 0, i)),
        ],
        out_shape=[
            jax.ShapeDtypeStruct((s, POOL_DIM), BF16),
            jax.ShapeDtypeStruct((MLA_HEADS, HEAD_PAD, s), BF16),
            jax.ShapeDtypeStruct((MLA_HEADS, s, HEAD_PAD), BF16),
            jax.ShapeDtypeStruct((MLA_HEADS, V_ROWS, s), BF16),
        ],
        scratch_shapes=[pltpu.VMEM((POOL_HALO, POOL_DIM), F32)],
        compiler_params=_params(("arbitrary",)),
        name="mla_pre",
    )(x, g.reshape(1, d), w_in, qa.reshape(1, -1), wq, kva.reshape(1, -1), wk, wv,
      qh.reshape(HEAD_PAD, 1), kh, kh.reshape(HEAD_PAD, 1), wp, ps.reshape(1, -1), cos_t, sin_t)


def _flash_body(qt_ref, qt_next_ref, k_ref, vt_ref, o_ref, s_refs, bm_refs, m_ref, acc_ref):
    i = pl.program_id(1)
    tq = qt_ref.shape[1]
    tk = ATTN_K_TILE
    m_ref[...] = jnp.full_like(m_ref, MASK_VALUE)
    acc_ref[...] = jnp.zeros_like(acc_ref)

    def scores(j, buf, c0=0, q_ref=qt_ref):
        off = pl.multiple_of(j * tk, tk)
        st = jnp.dot(k_ref[pl.ds(off, tk), :], q_ref[:, c0:], preferred_element_type=F32)
        s_refs[buf][:, c0:] = st
        bm_refs[buf][:, c0:] = jnp.max(st, axis=0, keepdims=True)

    def consume(j, buf, c0=0, diagonal=False):
        off = pl.multiple_of(j * tk, tk)
        st = s_refs[buf][:, c0:]
        if diagonal:
            row = lax.broadcasted_iota(jnp.int32, (tk, tk), 0)
            col = lax.broadcasted_iota(jnp.int32, (tk, tk), 1)
            square = jnp.where(row <= col, st[:, :tk], MASK_VALUE)
            st = square if tq - c0 == tk else jnp.concatenate([square, st[:, tk:]], axis=1)
            block_max = jnp.max(st, axis=0, keepdims=True)
        else:
            block_max = bm_refs[buf][:, c0:]
        m_old = m_ref[:, c0:]
        m_new = jnp.maximum(m_old, block_max)
        p = jnp.exp2(st - m_new).astype(BF16)
        alpha = jnp.exp2(m_old - m_new)
        vt = vt_ref[:, pl.ds(off, tk)]
        acc_ref[:, c0:] = acc_ref[:, c0:] * alpha + jnp.dot(vt, p, preferred_element_type=F32)
        m_ref[:, c0:] = m_new

    nb = ATTN_BLOCKS_PER_Q
    pairs = ((0, 1), (2, 3))

    @pl.when(i == 0)
    def _():
        scores(0, 0)
        scores(1, 1)

    def run_blocks(j, count):
        for ph in range(count // 2):
            drain, fill = pairs[ph % 2], pairs[(ph + 1) % 2]
            b = j + 2 * ph
            scores(b + 2, fill[0])
            consume(b, drain[0])
            scores(b + 3, fill[1])
            consume(b + 1, drain[1])

    full = nb * i
    trips = full // ATTN_BLOCKS_PER_TRIP

    def body(jj, carry):
        run_blocks(ATTN_BLOCKS_PER_TRIP * jj, ATTN_BLOCKS_PER_TRIP)
        return carry

    lax.fori_loop(0, trips, body, 0)

    def diagonal_blocks(j):
        for ph in range(nb // 2):
            drain, fill = pairs[ph % 2], pairs[(ph + 1) % 2]
            d = 2 * ph
            last = d + 2 >= nb
            frees_first_pair = d + 4 >= nb and not last
            if not last:
                scores(j + d + 2, fill[0], c0=(d + 2) * tk)
            consume(j + d, drain[0], c0=d * tk, diagonal=True)
            if frees_first_pair:
                scores(0, drain[0], q_ref=qt_next_ref)
            if not last:
                scores(j + d + 3, fill[1], c0=(d + 3) * tk)
            consume(j + d + 1, drain[1], c0=(d + 1) * tk, diagonal=True)
            if frees_first_pair:
                scores(1, drain[1], q_ref=qt_next_ref)
        acc = acc_ref[...]
        o_ref[...] = (acc[:V_HEAD_DIM] / acc[V_HEAD_DIM:V_HEAD_DIM + 1]).astype(BF16)

    for rest in range(0, ATTN_BLOCKS_PER_TRIP, nb):
        @pl.when(full - trips * ATTN_BLOCKS_PER_TRIP == rest)
        def _():
            run_blocks(trips * ATTN_BLOCKS_PER_TRIP, rest)
            diagonal_blocks(full)


def _flash(qt, k, vt):
    h, s, dp = k.shape
    tq, tk = ATTN_Q_TILE, ATTN_K_TILE
    assert tq == ATTN_BLOCKS_PER_Q * tk and ATTN_BLOCKS_PER_Q % 4 == 0
    n_buf = 4
    n_q = s // tq
    return pl.pallas_call(
        _flash_body,
        grid=(h, n_q),
        in_specs=[
            pl.BlockSpec((None, dp, tq), lambda hh, i: (hh, 0, i)),
            pl.BlockSpec((None, dp, tq), lambda hh, i: (hh, 0, jnp.minimum(i + 1, n_q - 1))),
            pl.BlockSpec((None, s, dp), lambda hh, i: (hh, 0, 0)),
            pl.BlockSpec((None, V_ROWS, s), lambda hh, i: (hh, 0, 0)),
        ],
        out_specs=pl.BlockSpec((V_HEAD_DIM, tq), lambda hh, i: (hh, i)),
        out_shape=jax.ShapeDtypeStruct((h * V_HEAD_DIM, s), BF16),
        scratch_shapes=[[pltpu.VMEM((tk, tq), F32)] * n_buf, [pltpu.VMEM((1, tq), F32)] * n_buf,
                        pltpu.VMEM((1, tq), F32), pltpu.VMEM((V_ROWS, tq), F32)],
        compiler_params=_params(("arbitrary", "arbitrary")),
        name="flash",
    )(qt, qt, k, vt)


def _conv_body(x_ref, g_ref, w_in_ref, cw_ref, w_out_ref, o_ref, tail_ref):
    i = pl.program_id(0)
    tm, d = x_ref.shape

    @pl.when(i == 0)
    def _():
        tail_ref[...] = jnp.zeros_like(tail_ref)

    x = x_ref[...]
    hn = _rms(x, g_ref[...]).astype(w_in_ref.dtype)
    z = jnp.dot(hn, w_in_ref[...], preferred_element_type=F32)
    gb = z[:, :d]
    u = z[:, d:2 * d] * z[:, 2 * d:]
    ue = jnp.concatenate([tail_ref[...], u], axis=0)
    tail_ref[...] = u[tm - CONV_HALO:, :]
    cw = cw_ref[...]
    y = cw[CONV_WIDTH - 1:CONV_WIDTH] * u
    for back in range(1, CONV_WIDTH):
        tap = CONV_WIDTH - 1 - back
        y = y + cw[tap:tap + 1] * pltpu.roll(ue, back, axis=0)[CONV_HALO:, :]
    y = (gb * y).astype(w_out_ref.dtype)
    o_ref[...] = x + jnp.dot(y, w_out_ref[...], preferred_element_type=F32)


def _conv_mixer(x, layer, g, idx, w_in, cw, w_out):
    s, d = x.shape
    tm = CONV_TILE
    row = lambda i: (i, 0)
    g = _rows(g)
    return pl.pallas_call(
        _conv_body,
        grid=(s // tm,),
        in_specs=[
            pl.BlockSpec((tm, d), row),
            _layer_spec(g, layer),
            _const_spec(w_in.shape),
            _layer_spec(cw, idx),
            _const_spec(w_out.shape),
        ],
        out_specs=pl.BlockSpec((tm, d), row),
        out_shape=jax.ShapeDtypeStruct((s, d), F32),
        scratch_shapes=[pltpu.VMEM((CONV_HALO, d), F32)],
        compiler_params=_params(("arbitrary",)),
        name="conv_mixer",
    )(x, g, w_in, cw, w_out)


def _pad_heads(w, head_dim, take_from, take_n):
    rows = w.shape[0]
    w = w.reshape(rows, MLA_HEADS, head_dim)[:, :, take_from:take_from + take_n]
    w = jnp.pad(w, ((0, 0), (0, 0), (0, HEAD_PAD - take_n)))
    return w.reshape(rows, MLA_HEADS * HEAD_PAD)


def _rope_tables(s):
    pos = jnp.arange(s, dtype=F32)
    inv_freq = ROPE_THETA ** (-jnp.arange(0, QK_ROPE_DIM, 2, dtype=F32) / QK_ROPE_DIM)
    ang = inv_freq[:, None] * pos[None, :]
    return jnp.cos(ang), jnp.sin(ang)


def kernel(x, ffn1_norm, ffn1_w_gate, ffn1_w_up, ffn1_w_down, mix_norm, ffn2_norm, ffn2_w_gate, ffn2_w_up, ffn2_w_down, a_w_in, a_q_a_norm, a_w_q_up, a_kv_a_norm, a_w_kv_up, a_q_head_norm, a_k_head_norm, a_w_pool, a_pool_scale, a_w_out, c_w_in, c_conv_w, c_w_out):
    b, s, d = x.shape
    assert b == 1 and d == D_MODEL and s % TOKEN_TILE == 0 and s % ATTN_Q_TILE == 0
    h = x.reshape(s, d)
    cos_t, sin_t = _rope_tables(s)
    head_pad = ((0, 0), (0, HEAD_PAD - QK_HEAD_DIM))

    ffn_stacks = []
    for layer in range(DEPTH):
        ffn_stacks.append((layer, ffn1_norm, ffn1_w_gate, ffn1_w_up, ffn1_w_down))
        ffn_stacks.append((layer, ffn2_norm, ffn2_w_gate, ffn2_w_up, ffn2_w_down))
    ffn_weights = tuple((w, 0) for w in ffn_stacks[0][2:])

    def ffn_step(h, n, ffn_weights, mix=None, also_cast=()):
        layer, norm = ffn_stacks[n][:2]
        cast_next = ()
        if n + 1 < len(ffn_stacks):
            nxt = ffn_stacks[n + 1]
            cast_next = tuple((w, nxt[0]) for w in nxt[2:])
        h, cast = _ffn(h, norm[layer].reshape(1, d), ffn_weights, mix=mix,
                       cast_next=cast_next + tuple(also_cast))
        return h, cast[:len(cast_next)], cast[len(cast_next):]

    for layer in range(DEPTH):
        i = layer // 2
        conv_f32 = ((c_w_in, i), (c_w_out, i)) if layer % 2 == 1 else ()
        h, ffn_weights, conv_weights = ffn_step(h, 2 * layer, ffn_weights, also_cast=conv_f32)
        if layer % 2 == 0:
            w_in = a_w_in[i]
            c3 = POOL_DIM + Q_LORA_RANK + KV_LORA_RANK
            w_in_p = jnp.concatenate(
                [w_in[:, :c3], jnp.zeros((d, QK_NOPE_DIM), F32), w_in[:, c3:],
                 jnp.zeros((d, HEAD_PAD - QK_HEAD_DIM), F32)], axis=-1).astype(BF16)
            wq = _pad_heads(a_w_q_up[i], QK_HEAD_DIM, 0, QK_HEAD_DIM).astype(BF16)
            wk = _pad_heads(a_w_kv_up[i], QK_NOPE_DIM + V_HEAD_DIM, 0, QK_NOPE_DIM).astype(BF16)
            wv = _pad_heads(a_w_kv_up[i], QK_NOPE_DIM + V_HEAD_DIM, QK_NOPE_DIM, V_HEAD_DIM).astype(BF16)
            qh = jnp.pad(a_q_head_norm[i].reshape(1, -1), head_pad)
            kh = jnp.pad(a_k_head_norm[i].reshape(1, -1), head_pad)
            pool, qt, k, vt = _mla_pre(h, mix_norm[layer], w_in_p, a_q_a_norm[i], wq, a_kv_a_norm[i],
                                      wk, wv, qh, kh, a_w_pool[i].astype(BF16), a_pool_scale[i],
                                      cos_t, sin_t)
            attn_t = _flash(qt, k, vt)
            mix = (pool, attn_t, i, a_w_out)
        else:
            h = _conv_mixer(h, layer, mix_norm, i, conv_weights[0], c_conv_w, conv_weights[1])
            mix = None
        h, ffn_weights, _ = ffn_step(h, 2 * layer + 1, ffn_weights, mix=mix)
    return h.reshape(b, s, d)
```
